```python
import jax, jax.numpy as jnp
from jax import lax
import numpy as np

D_MODEL = 2048
BATCH = 4
SEQ = 2048
DEPTH = 4

GRID_W = 64
CTX_LEN = 256
HEAD_DIM = 128
BLOCK_Q = 128
ROPE_BASE = 10000.0
RMS_EPS = 1e-6
NEG_INF = -1e30

A_HEADS = 8
A_KV_HEADS = 2
WINDOW = 128

B_HEADS = 8
MLA_Q_RANK = 512
MLA_KV_RANK = 256
MLA_NOPE_DIM = 128
MLA_ROPE_DIM = 64
MLA_V_DIM = 128

C_HEADS = 8
C_KV_HEADS = 2

D_HEADS = 8
NB_ROWS = 8
NB_COLS = 16

N_EXPERTS = 16
CAPACITY_FACTOR = 2
D_EXPERT = 1024

A_Q = A_HEADS * HEAD_DIM
A_KV = A_KV_HEADS * HEAD_DIM
AB_SPLITS = (A_Q, A_Q + A_KV, A_Q + 2 * A_KV, A_Q + 2 * A_KV + MLA_Q_RANK, A_Q + 2 * A_KV + MLA_Q_RANK + MLA_KV_RANK)
AB_IN = A_Q + 2 * A_KV + MLA_Q_RANK + MLA_KV_RANK + MLA_ROPE_DIM
C_Q = C_HEADS * HEAD_DIM
C_KV = C_KV_HEADS * HEAD_DIM
D_QKV = D_HEADS * HEAD_DIM
CD_SPLITS = (C_Q, C_Q + C_KV, C_Q + 2 * C_KV, C_Q + 2 * C_KV + D_QKV, C_Q + 2 * C_KV + 2 * D_QKV)
CD_IN = C_Q + 2 * C_KV + 3 * D_QKV
AB_OUT = A_HEADS * HEAD_DIM + B_HEADS * MLA_V_DIM
CD_OUT = C_HEADS * HEAD_DIM + D_HEADS * HEAD_DIM

kernel_name = 'hybrid_flow_backbone'


def rmsnorm(x, g):
    x32 = x.astype(jnp.float32)
    y = x32 * lax.rsqrt(jnp.mean(x32 * x32, axis=-1, keepdims=True) + RMS_EPS)
    return (y * g.astype(jnp.float32)).astype(x.dtype)


def rope_1d(x, pos):
    d = x.shape[-1]
    inv = ROPE_BASE ** (-jnp.arange(0, d, 2, dtype=jnp.float32) / d)
    ang = pos.astype(jnp.float32)[:, None] * inv[None, :]
    cos = jnp.cos(ang)[:, None, :].astype(x.dtype)
    sin = jnp.sin(ang)[:, None, :].astype(x.dtype)
    x1, x2 = x[..., : d // 2], x[..., d // 2:]
    return jnp.concatenate([x1 * cos - x2 * sin, x2 * cos + x1 * sin], axis=-1)


def rope_2d(x, rows, cols):
    half = x.shape[-1] // 2
    return jnp.concatenate([rope_1d(x[..., :half], rows), rope_1d(x[..., half:], cols)], axis=-1)


def ctx_attend(q, k, v, scale, sink=None):
    bsz, t = q.shape[:2]
    s = jnp.einsum('btkgd,bjkd->bkgtj', q, k).astype(jnp.float32) * scale
    if sink is None:
        p = jax.nn.softmax(s, axis=-1)
    else:
        kh, g = q.shape[2], q.shape[3]
        sk = jnp.broadcast_to(sink.astype(jnp.float32).reshape(kh, g, 1, 1), s.shape[:-1] + (1,))
        p = jax.nn.softmax(jnp.concatenate([s, sk], axis=-1), axis=-1)[..., :-1]
    out = jnp.einsum('bkgtj,bjkv->btkgv', p.astype(v.dtype), v)
    return out.reshape(bsz, t, -1)


def window_attend(q, k, v, kc, vc, sink, scale):
    bsz, s_len, kh, g, d = q.shape
    nb = s_len // BLOCK_Q
    span = BLOCK_Q + 2 * WINDOW
    kp = jnp.pad(k, ((0, 0), (WINDOW, WINDOW), (0, 0), (0, 0)))
    vp = jnp.pad(v, ((0, 0), (WINDOW, WINDOW), (0, 0), (0, 0)))
    idx = jnp.arange(nb)[:, None] * BLOCK_Q + jnp.arange(span)[None, :]
    kb = kp[:, idx]
    vb = vp[:, idx]
    qpos = jnp.arange(nb)[:, None] * BLOCK_Q + jnp.arange(BLOCK_Q)[None, :]
    kpos = idx - WINDOW
    valid = ((jnp.abs(qpos[:, :, None] - kpos[:, None, :]) <= WINDOW)
             & (kpos[:, None, :] >= 0) & (kpos[:, None, :] < s_len))
    qb = q.reshape(bsz, nb, BLOCK_Q, kh, g, d)
    s_loc = jnp.einsum('bnqkgd,bnjkd->bnkgqj', qb, kb).astype(jnp.float32) * scale
    s_loc = jnp.where(valid[None, :, None, None], s_loc, NEG_INF)
    s_ctx = jnp.einsum('bnqkgd,bjkd->bnkgqj', qb, kc).astype(jnp.float32) * scale
    sk = jnp.broadcast_to(sink.astype(jnp.float32).reshape(kh, g, 1, 1), s_loc.shape[:-1] + (1,))
    p = jax.nn.softmax(jnp.concatenate([s_loc, s_ctx, sk], axis=-1), axis=-1)[..., :-1].astype(v.dtype)
    out = (jnp.einsum('bnkgqj,bnjkv->bnqkgv', p[..., :span], vb)
           + jnp.einsum('bnkgqj,bjkv->bnqkgv', p[..., span:], vc))
    return out.reshape(bsz, s_len, -1)


def dense_attend(q, k, v, kc, vc, scale):
    bsz, s_len, kh, g, d = q.shape
    nb = s_len // BLOCK_Q
    k_all = jnp.concatenate([k, kc], axis=1)
    v_all = jnp.concatenate([v, vc], axis=1)
    qb = jnp.moveaxis(q.reshape(bsz, nb, BLOCK_Q, kh, g, d), 1, 0)

    def one_block(qi):
        s = jnp.einsum('bqkgd,bjkd->bkgqj', qi, k_all).astype(jnp.float32) * scale
        p = jax.nn.softmax(s, axis=-1).astype(v_all.dtype)
        return jnp.einsum('bkgqj,bjkv->bqkgv', p, v_all)

    out = lax.map(one_block, qb)
    return jnp.moveaxis(out, 0, 1).reshape(bsz, s_len, -1)


def neighbourhood_attend(q, k, v, kc, vc, rpb, scale):
    bsz, s_len, h, d = q.shape
    rows_n = s_len // GRID_W
    kr = min(NB_ROWS, rows_n)
    qg = jnp.moveaxis(q.reshape(bsz, rows_n, GRID_W, h, d), 1, 0)
    kg = k.reshape(bsz, rows_n, GRID_W, h, d)
    vg = v.reshape(bsz, rows_n, GRID_W, h, d)
    r_idx = jnp.arange(rows_n)
    row_start = jnp.clip(r_idx - kr // 2, 0, rows_n - kr)
    col = jnp.arange(GRID_W)
    col_start = jnp.clip(col - NB_COLS // 2, 0, GRID_W - NB_COLS)
    key_row_off = jnp.repeat(jnp.arange(kr), GRID_W)
    key_col = jnp.tile(col, kr)
    col_mask = ((key_col[None, :] >= col_start[:, None])
                & (key_col[None, :] < col_start[:, None] + NB_COLS))
    dc = jnp.clip(key_col[None, :] - col[:, None] + NB_COLS - 1, 0, 2 * NB_COLS - 2)
    n_loc = kr * GRID_W

    def one_row(args):
        qr, r, rs = args
        kband = lax.dynamic_slice_in_dim(kg, rs, kr, axis=1).reshape(bsz, n_loc, h, d)
        vband = lax.dynamic_slice_in_dim(vg, rs, kr, axis=1).reshape(bsz, n_loc, h, d)
        dr = rs + key_row_off - r + NB_ROWS - 1
        bias = rpb[:, dr[None, :], dc].astype(jnp.float32)
        s_loc = jnp.einsum('bqhd,bjhd->bhqj', qr, kband).astype(jnp.float32) * scale + bias[None]
        s_loc = jnp.where(col_mask[None, None], s_loc, NEG_INF)
        s_ctx = jnp.einsum('bqhd,bjhd->bhqj', qr, kc).astype(jnp.float32) * scale
        p = jax.nn.softmax(jnp.concatenate([s_loc, s_ctx], axis=-1), axis=-1).astype(v.dtype)
        return (jnp.einsum('bhqj,bjhv->bqhv', p[..., :n_loc], vband)
                + jnp.einsum('bhqj,bjhv->bqhv', p[..., n_loc:], vc))

    out = lax.map(one_row, (qg, r_idx, row_start))
    return jnp.moveaxis(out, 0, 1).reshape(bsz, s_len, h * d)


def project_ab(h, rows, cols, rotate, w_in, q_norm, w_uq, kv_norm, w_ukv):
    bsz, t, _ = h.shape
    aq, ak, av, cq, ckv, k_rope = jnp.split(h @ w_in, AB_SPLITS, axis=-1)
    aq = aq.reshape(bsz, t, A_HEADS, HEAD_DIM)
    ak = ak.reshape(bsz, t, A_KV_HEADS, HEAD_DIM)
    av = av.reshape(bsz, t, A_KV_HEADS, HEAD_DIM)
    bq = (rmsnorm(cq, q_norm) @ w_uq).reshape(bsz, t, B_HEADS, MLA_NOPE_DIM + MLA_ROPE_DIM)
    bkv = (rmsnorm(ckv, kv_norm) @ w_ukv).reshape(bsz, t, B_HEADS, MLA_NOPE_DIM + MLA_V_DIM)
    bq_nope, bq_rope = bq[..., :MLA_NOPE_DIM], bq[..., MLA_NOPE_DIM:]
    bk_nope, bv = bkv[..., :MLA_NOPE_DIM], bkv[..., MLA_NOPE_DIM:]
    k_rope = k_rope[:, :, None, :]
    if rotate:
        aq = rope_2d(aq, rows, cols)
        ak = rope_2d(ak, rows, cols)
        bq_rope = rope_2d(bq_rope, rows, cols)
        k_rope = rope_2d(k_rope, rows, cols)
    aq = aq.reshape(bsz, t, A_KV_HEADS, A_HEADS // A_KV_HEADS, HEAD_DIM)
    bq = jnp.concatenate([bq_nope, bq_rope], axis=-1)[:, :, :, None, :]
    bk = jnp.concatenate([bk_nope, jnp.broadcast_to(k_rope, (bsz, t, B_HEADS, MLA_ROPE_DIM))], axis=-1)
    return aq, ak, av, bq, bk, bv


def mixer_ab(hc, hl, rows, cols, need_ctx, w_in, sink, q_norm, w_uq, kv_norm, w_ukv, w_out):
    weights = (w_in, q_norm, w_uq, kv_norm, w_ukv)
    caq, cak, cav, cbq, cbk, cbv = project_ab(hc, None, None, False, *weights)
    laq, lak, lav, lbq, lbk, lbv = project_ab(hl, rows, cols, True, *weights)
    scale_a = HEAD_DIM ** -0.5
    scale_b = (MLA_NOPE_DIM + MLA_ROPE_DIM) ** -0.5
    y_lat = jnp.concatenate([window_attend(laq, lak, lav, cak, cav, sink, scale_a),
                             dense_attend(lbq, lbk, lbv, cbk, cbv, scale_b)], axis=-1) @ w_out
    y_ctx = None
    if need_ctx:
        y_ctx = jnp.concatenate([ctx_attend(caq, cak, cav, scale_a, sink),
                                 ctx_attend(cbq, cbk, cbv, scale_b)], axis=-1) @ w_out
    return y_ctx, y_lat


def project_cd(h, rows, cols, rotate, w_in, q_norm, k_norm):
    bsz, t, _ = h.shape
    cq, ck, cv, dq, dk, dv = jnp.split(h @ w_in, CD_SPLITS, axis=-1)
    cq = rmsnorm(cq.reshape(bsz, t, C_HEADS, HEAD_DIM), q_norm)
    ck = rmsnorm(ck.reshape(bsz, t, C_KV_HEADS, HEAD_DIM), k_norm)
    cv = cv.reshape(bsz, t, C_KV_HEADS, HEAD_DIM)
    if rotate:
        cq = rope_2d(cq, rows, cols)
        ck = rope_2d(ck, rows, cols)
    cq = cq.reshape(bsz, t, C_KV_HEADS, C_HEADS // C_KV_HEADS, HEAD_DIM)
    dq = dq.reshape(bsz, t, D_HEADS, HEAD_DIM)
    dk = dk.reshape(bsz, t, D_HEADS, HEAD_DIM)
    dv = dv.reshape(bsz, t, D_HEADS, HEAD_DIM)
    return cq, ck, cv, dq, dk, dv


def mixer_cd(hc, hl, rows, cols, need_ctx, w_in, q_norm, k_norm, rpb, w_out):
    ccq, cck, ccv, cdq, cdk, cdv = project_cd(hc, None, None, False, w_in, q_norm, k_norm)
    lcq, lck, lcv, ldq, ldk, ldv = project_cd(hl, rows, cols, True, w_in, q_norm, k_norm)
    scale = HEAD_DIM ** -0.5
    y_lat = jnp.concatenate([dense_attend(lcq, lck, lcv, cck, ccv, scale),
                             neighbourhood_attend(ldq, ldk, ldv, cdk, cdv, rpb, scale)], axis=-1) @ w_out
    y_ctx = None
    if need_ctx:
        y_ctx = jnp.concatenate([ctx_attend(ccq, cck, ccv, scale),
                                 ctx_attend(cdq[:, :, :, None, :], cdk, cdv, scale)], axis=-1) @ w_out
    return y_ctx, y_lat


def ec_moe(h, w_router, w_gate, w_up, w_down):
    bsz, t, d = h.shape
    cap = CAPACITY_FACTOR * t // N_EXPERTS
    aff = jax.nn.softmax(jnp.einsum('btd,de->bte', h, w_router).astype(jnp.float32), axis=-1)
    g, idx = lax.top_k(jnp.swapaxes(aff, 1, 2), cap)
    xin = jax.vmap(lambda hb, ib: hb[ib])(h, idx)
    hid = (jax.nn.silu(jnp.einsum('becd,edf->becf', xin, w_gate))
           * jnp.einsum('becd,edf->becf', xin, w_up))
    out = jnp.einsum('becf,efd->becd', hid, w_down) * g[..., None].astype(h.dtype)
    return jax.vmap(lambda ob, ib: jnp.zeros((t, d), ob.dtype).at[ib.reshape(-1)].add(ob.reshape(-1, d)))(out, idx)


def setup_inputs(seed: int = 0) -> dict:
    key = jax.random.key(seed)
    ks = jax.random.split(key, 25)
    n_ab = (DEPTH + 1) // 2
    n_cd = DEPTH // 2

    def nrm(k, shape, scale):
        return jax.random.normal(k, shape, jnp.float32) * scale

    def gain(k, shape):
        return 1.0 + 0.1 * jax.random.normal(k, shape, jnp.float32)

    return {
        'x': nrm(ks[0], (BATCH, SEQ, D_MODEL), 1.0),
        'c': nrm(ks[1], (BATCH, D_MODEL), 1.0),
        'ctx': nrm(ks[2], (BATCH, CTX_LEN, D_MODEL), 1.0),
        'c_ctx': nrm(ks[3], (D_MODEL,), 1.0),
        'w_ada': nrm(ks[4], (DEPTH, D_MODEL, 6 * D_MODEL), 0.5 * D_MODEL ** -0.5),
        'b_ada': nrm(ks[5], (DEPTH, 6 * D_MODEL), 0.02),
        'norm1': gain(ks[6], (DEPTH, D_MODEL)),
        'norm2': gain(ks[7], (DEPTH, D_MODEL)),
        'ab_w_in': nrm(ks[8], (n_ab, D_MODEL, AB_IN), D_MODEL ** -0.5),
        'ab_sink': nrm(ks[9], (n_ab, A_HEADS), 0.5),
        'ab_q_norm': gain(ks[10], (n_ab, MLA_Q_RANK)),
        'ab_w_uq': nrm(ks[11], (n_ab, MLA_Q_RANK, B_HEADS * (MLA_NOPE_DIM + MLA_ROPE_DIM)), MLA_Q_RANK ** -0.5),
        'ab_kv_norm': gain(ks[12], (n_ab, MLA_KV_RANK)),
        'ab_w_ukv': nrm(ks[13], (n_ab, MLA_KV_RANK, B_HEADS * (MLA_NOPE_DIM + MLA_V_DIM)), MLA_KV_RANK ** -0.5),
        'ab_w_out': nrm(ks[14], (n_ab, AB_OUT, D_MODEL), AB_OUT ** -0.5),
        'cd_w_in': nrm(ks[15], (n_cd, D_MODEL, CD_IN), D_MODEL ** -0.5),
        'cd_q_norm': gain(ks[16], (n_cd, HEAD_DIM)),
        'cd_k_norm': gain(ks[17], (n_cd, HEAD_DIM)),
        'cd_rpb': nrm(ks[18], (n_cd, D_HEADS, 2 * NB_ROWS - 1, 2 * NB_COLS - 1), 0.1),
        'cd_w_out': nrm(ks[19], (n_cd, CD_OUT, D_MODEL), CD_OUT ** -0.5),
        'w_router': nrm(ks[20], (DEPTH, D_MODEL, N_EXPERTS), D_MODEL ** -0.5),
        'w_gate': nrm(ks[21], (DEPTH, N_EXPERTS, D_MODEL, D_EXPERT), D_MODEL ** -0.5),
        'w_up': nrm(ks[22], (DEPTH, N_EXPERTS, D_MODEL, D_EXPERT), D_MODEL ** -0.5),
        'w_down': nrm(ks[23], (DEPTH, N_EXPERTS, D_EXPERT, D_MODEL), D_EXPERT ** -0.5),
        'final_norm': gain(ks[24], (D_MODEL,)),
    }


def reference(x, c, ctx, c_ctx, w_ada, b_ada, norm1, norm2, ab_w_in, ab_sink, ab_q_norm, ab_w_uq,
              ab_kv_norm, ab_w_ukv, ab_w_out, cd_w_in, cd_q_norm, cd_k_norm, cd_rpb, cd_w_out,
              w_router, w_gate, w_up, w_down, final_norm):
    s_len = x.shape[1]
    t = jnp.arange(s_len)
    rows, cols = t // GRID_W, t % GRID_W
    xl, xc = x, ctx
    for l in range(DEPTH):
        need_ctx = l < DEPTH - 1
        mod_l = (jax.nn.silu(c) @ w_ada[l] + b_ada[l])[:, None, :]
        mod_c = (jax.nn.silu(c_ctx) @ w_ada[l] + b_ada[l])[None, None, :]
        sh1, sc1, g1, sh2, sc2, g2 = jnp.split(mod_l, 6, axis=-1)
        csh1, csc1, cg1, csh2, csc2, cg2 = jnp.split(mod_c, 6, axis=-1)
        hl = rmsnorm(xl, norm1[l]) * (1.0 + sc1) + sh1
        hc = rmsnorm(xc, norm1[l]) * (1.0 + csc1) + csh1
        i = l // 2
        if l % 2 == 0:
            y_ctx, y_lat = mixer_ab(hc, hl, rows, cols, need_ctx, ab_w_in[i], ab_sink[i], ab_q_norm[i],
                                    ab_w_uq[i], ab_kv_norm[i], ab_w_ukv[i], ab_w_out[i])
        else:
            y_ctx, y_lat = mixer_cd(hc, hl, rows, cols, need_ctx, cd_w_in[i], cd_q_norm[i], cd_k_norm[i],
                                    cd_rpb[i], cd_w_out[i])
        xl = xl + g1 * y_lat
        hl = rmsnorm(xl, norm2[l]) * (1.0 + sc2) + sh2
        xl = xl + g2 * ec_moe(hl, w_router[l], w_gate[l], w_up[l], w_down[l])
        if need_ctx:
            xc = xc + cg1 * y_ctx
            hc = rmsnorm(xc, norm2[l]) * (1.0 + csc2) + csh2
            xc = xc + cg2 * ec_moe(hc, w_router[l], w_gate[l], w_up[l], w_down[l])
    return rmsnorm(xl, final_norm)
```

```python
import functools

import jax
import jax.numpy as jnp
from jax import lax
from jax.experimental import pallas as pl
from jax.experimental.pallas import tpu as pltpu

F32 = jnp.float32
BF16 = jnp.bfloat16
I32 = jnp.int32

D = 2048
B = 4
S = 2048
L = 256
T = S + L
R = B * T
DEPTH = 4
GRID_W = 64
HD = 128
ROPE_BASE = 10000.0
EPS = 1e-6
NEG = -1e30
WINDOW = 128
BQ = 128
NB_ROWS = 8
NB_COLS = 16
E = 16
CAP_L = 2 * S // E
CAP_C = 2 * L // E
SLOTS = CAP_L + CAP_C
FE = 1024
FFN_ROWS = 384
MLA_Q = 512
MLA_KV = 256
AB_IN = 2368
CD_IN = 4608

TM = 256
TILES = T // TM
LAT_TILES = S // TM
CTX_ROW = B
VMEM_LIMIT = 56 * 1024 * 1024


def _cp(*sem):
    return pltpu.CompilerParams(dimension_semantics=sem, vmem_limit_bytes=VMEM_LIMIT)


def _mod_row(tile):
    b = tile // TILES
    tt = tile % TILES
    return jnp.where(tt == LAT_TILES, CTX_ROW, b)


def _dot(a, b):
    return jnp.dot(a, b, preferred_element_type=F32)


def _dot_nt(a, b):
    return lax.dot_general(a, b, (((1,), (1,)), ((), ())), preferred_element_type=F32)


def _ada_body(c_ref, w_ref, b_ref, o_ref):
    c = c_ref[...]
    a = c * jax.nn.sigmoid(c)
    o_ref[0] = _dot(a.astype(BF16), w_ref[0].astype(BF16)) + b_ref[0]


def ada_modulation(c8, w_ada, b_ada):
    tn = 1024
    n = 6 * D
    return pl.pallas_call(
        _ada_body,
        grid=(DEPTH, n // tn),
        in_specs=[
            pl.BlockSpec((8, D), lambda l, j: (0, 0)),
            pl.BlockSpec((1, D, tn), lambda l, j: (l, 0, j)),
            pl.BlockSpec((1, 1, tn), lambda l, j: (l, 0, j)),
        ],
        out_specs=pl.BlockSpec((1, 8, tn), lambda l, j: (l, 0, j)),
        out_shape=jax.ShapeDtypeStruct((DEPTH, 8, n), F32),
        compiler_params=_cp("parallel", "parallel"),
        name="ada_modulation",
    )(c8, w_ada, b_ada.reshape(DEPTH, 1, n))


def _rms(x, gain):
    return x * lax.rsqrt(jnp.mean(x * x, axis=-1, keepdims=True) + EPS) * gain


def _normmod_body(x_ref, gain_ref, sc_ref, sh_ref, o_ref):
    row = _mod_row(pl.program_id(0))
    y = _rms(x_ref[...], gain_ref[...])
    o_ref[...] = (y * (1.0 + sc_ref[0, pl.ds(row, 1), :]) + sh_ref[0, pl.ds(row, 1), :]).astype(o_ref.dtype)


def norm_modulate(x2d, gain, modt, l, chunk_sc, chunk_sh):
    return pl.pallas_call(
        _normmod_body,
        grid=(R // TM,),
        in_specs=[
            pl.BlockSpec((TM, D), lambda i: (i, 0)),
            pl.BlockSpec((1, D), lambda i: (0, 0)),
            pl.BlockSpec((1, 8, D), lambda i: (l, 0, chunk_sc)),
            pl.BlockSpec((1, 8, D), lambda i: (l, 0, chunk_sh)),
        ],
        out_specs=pl.BlockSpec((TM, D), lambda i: (i, 0)),
        out_shape=jax.ShapeDtypeStruct((R, D), BF16),
        compiler_params=_cp("parallel"),
        name="norm_modulate",
    )(x2d, gain.reshape(1, D), modt, modt)


def _mm_body(a_ref, b_ref, o_ref):
    o_ref[...] = _dot(a_ref[...].astype(BF16), b_ref[...].astype(BF16)).astype(o_ref.dtype)


def matmul(a, b, out_dtype, tm, tn):
    m, k = a.shape
    n = b.shape[1]
    return pl.pallas_call(
        _mm_body,
        grid=(n // tn, m // tm),
        in_specs=[
            pl.BlockSpec((tm, k), lambda j, i: (i, 0)),
            pl.BlockSpec((k, tn), lambda j, i: (0, j)),
        ],
        out_specs=pl.BlockSpec((tm, tn), lambda j, i: (i, j)),
        out_shape=jax.ShapeDtypeStruct((m, n), out_dtype),
        compiler_params=_cp("parallel", "parallel"),
        name="matmul",
    )(a, b)


def _mm_res_body(a_ref, b_ref, x_ref, g_ref, o_ref):
    row = _mod_row(pl.program_id(1))
    y = _dot(a_ref[...], b_ref[...])
    o_ref[...] = x_ref[...] + g_ref[0, pl.ds(row, 1), :] * y


def matmul_gated_residual(a, b, x2d, modt, l, chunk):
    tn = 1024
    k = a.shape[1]
    per = D // tn
    return pl.pallas_call(
        _mm_res_body,
        grid=(D // tn, R // TM),
        in_specs=[
            pl.BlockSpec((TM, k), lambda j, i: (i, 0)),
            pl.BlockSpec((k, tn), lambda j, i: (0, j)),
            pl.BlockSpec((TM, tn), lambda j, i: (i, j)),
            pl.BlockSpec((1, 8, tn), lambda j, i: (l, 0, chunk * per + j)),
        ],
        out_specs=pl.BlockSpec((TM, tn), lambda j, i: (i, j)),
        out_shape=jax.ShapeDtypeStruct((R, D), F32),
        compiler_params=_cp("parallel", "parallel"),
        name="matmul_gated_residual",
    )(a, b, x2d, modt)


def _softmax_pv(parts, sink, rows):
    m = parts[0][0].max(axis=-1, keepdims=True)
    for s, _ in parts[1:]:
        m = jnp.maximum(m, s.max(axis=-1, keepdims=True))
    if sink is not None:
        m = jnp.maximum(m, sink)
    den = jnp.zeros((rows, 1), F32) if sink is None else jnp.exp(sink - m)
    acc = None
    for s, v in parts:
        p = jnp.exp(s - m)
        den = den + p.sum(axis=-1, keepdims=True)
        pv = _dot(p.astype(BF16), v)
        acc = pv if acc is None else acc + pv
    return acc / den


def _dense_attn_body(*refs, g, tq, dk, dv, scale, has_sink):
    if has_sink:
        q_ref, k_ref, v_ref, sink_ref, _, o_ref = refs
        sink = sink_ref[0]
    else:
        q_ref, k_ref, v_ref, _, o_ref = refs
        sink = None
    q = jnp.concatenate([q_ref[0, :, j * dk:(j + 1) * dk] for j in range(g)], axis=0) if g > 1 else q_ref[0]
    s = _dot_nt(q, k_ref[0]) * scale
    o = _softmax_pv([(s, v_ref[0])], sink, g * tq)
    for j in range(g):
        o_ref[0, :, j * dv:(j + 1) * dv] = o[j * tq:(j + 1) * tq].astype(o_ref.dtype)


def dense_attention(att, q, k, v, *, kvh, g, dk, dv, tq, q_off, n_q, nk, k_blk, col_off, scale, sink_col=None):
    in_specs = [
        pl.BlockSpec((1, tq, g * dk), lambda b, h, i: (b, i + q_off, h)),
        pl.BlockSpec((1, nk, dk), lambda b, h, i: (b, k_blk, h)),
        pl.BlockSpec((1, nk, dv), lambda b, h, i: (b, k_blk, h)),
    ]
    args = [q, k, v]
    if sink_col is not None:
        in_specs.append(pl.BlockSpec((1, g * tq, 1), lambda b, h, i: (h, 0, 0)))
        args.append(sink_col)
    in_specs.append(pl.BlockSpec(memory_space=pl.ANY))
    args.append(att)
    return pl.pallas_call(
        functools.partial(_dense_attn_body, g=g, tq=tq, dk=dk, dv=dv, scale=scale,
                          has_sink=sink_col is not None),
        grid=(B, kvh, n_q),
        in_specs=in_specs,
        out_specs=pl.BlockSpec((1, tq, g * dv), lambda b, h, i: (b, i + q_off, h + col_off)),
        out_shape=jax.ShapeDtypeStruct(att.shape, att.dtype),
        input_output_aliases={len(args) - 1: 0},
        compiler_params=_cp("parallel", "parallel", "parallel"),
        name="dense_attention",
    )(*args)


def _window_attn_body(q_ref, kp_ref, kc_ref, kn_ref, kx_ref, vp_ref, vc_ref, vn_ref, vx_ref, sink_ref,
                      _, o_ref, *, g, scale):
    n = pl.program_id(2)
    nb = S // BQ
    rows = g * BQ
    q = jnp.concatenate([q_ref[0, :, j * HD:(j + 1) * HD] for j in range(g)], axis=0)
    kl = jnp.concatenate([kp_ref[0], kc_ref[0], kn_ref[0]], axis=0)
    vl = jnp.concatenate([vp_ref[0], vc_ref[0], vn_ref[0]], axis=0)
    s_loc = _dot_nt(q, kl) * scale
    qi = lax.broadcasted_iota(I32, (rows, 3 * BQ), 0) % BQ + BQ
    kj = lax.broadcasted_iota(I32, (rows, 3 * BQ), 1)
    lo = jnp.where(n > 0, 0, BQ)
    hi = jnp.where(n < nb - 1, 3 * BQ, 2 * BQ)
    valid = (jnp.abs(qi - kj) <= WINDOW) & (kj >= lo) & (kj < hi)
    s_loc = jnp.where(valid, s_loc, NEG)
    s_ctx = _dot_nt(q, kx_ref[0]) * scale
    o = _softmax_pv([(s_loc, vl), (s_ctx, vx_ref[0])], sink_ref[0], rows)
    for j in range(g):
        o_ref[0, :, j * HD:(j + 1) * HD] = o[j * BQ:(j + 1) * BQ].astype(o_ref.dtype)


def window_attention(att, q, k, v, sink_col, *, kvh, g, scale):
    nb = S // BQ
    ctx_blk = S // L

    def kv_specs():
        return [
            pl.BlockSpec((1, BQ, HD), lambda b, h, n: (b, jnp.maximum(n - 1, 0), h)),
            pl.BlockSpec((1, BQ, HD), lambda b, h, n: (b, n, h)),
            pl.BlockSpec((1, BQ, HD), lambda b, h, n: (b, jnp.minimum(n + 1, nb - 1), h)),
            pl.BlockSpec((1, L, HD), lambda b, h, n: (b, ctx_blk, h)),
        ]

    return pl.pallas_call(
        functools.partial(_window_attn_body, g=g, scale=scale),
        grid=(B, kvh, nb),
        in_specs=[pl.BlockSpec((1, BQ, g * HD), lambda b, h, n: (b, n, h))] + kv_specs() + kv_specs()
        + [pl.BlockSpec((1, g * BQ, 1), lambda b, h, n: (h, 0, 0)), pl.BlockSpec(memory_space=pl.ANY)],
        out_specs=pl.BlockSpec((1, BQ, g * HD), lambda b, h, n: (b, n, h)),
        out_shape=jax.ShapeDtypeStruct(att.shape, att.dtype),
        input_output_aliases={10: 0},
        compiler_params=_cp("parallel", "parallel", "parallel"),
        name="window_attention",
    )(q, k, k, k, k, v, v, v, v, sink_col, att)


ROWS_N = S // GRID_W
NBR_G = 4
NBR_KR = NBR_G + NB_ROWS
NBR_GROUPS = ROWS_N // NBR_G
NBR_PATTERN_GROUPS = (0, 1, NBR_GROUPS - 1)


def _nbr_window_start(grp):
    return NBR_G * grp - NB_ROWS // 2


def _nbr_bias_body(rpb_ref, o_ref):
    h = pl.program_id(0)
    n_dc = 2 * NB_COLS - 1
    n_dr = 2 * NB_ROWS - 1
    qc = lax.broadcasted_iota(I32, (GRID_W, GRID_W), 0)
    kc = lax.broadcasted_iota(I32, (GRID_W, GRID_W), 1)
    dc = jnp.clip(kc - qc + NB_COLS - 1, 0, n_dc - 1)
    cs = jnp.clip(qc - NB_COLS // 2, 0, GRID_W - NB_COLS)
    col_ok = (kc >= cs) & (kc < cs + NB_COLS)
    tiles = [jnp.zeros((GRID_W, GRID_W), F32) for _ in range(n_dr)]
    for t in range(n_dc):
        hit = dc == t
        for dr in range(n_dr):
            tiles[dr] = jnp.where(hit, rpb_ref[h, dr * n_dc + t], tiles[dr])
    tiles = [jnp.where(col_ok, x, NEG) for x in tiles]
    masked = jnp.full((GRID_W, GRID_W), NEG, F32)
    for p, grp in enumerate(NBR_PATTERN_GROUPS):
        start = min(max(_nbr_window_start(grp), 0), ROWS_N - NBR_KR)
        for ri in range(NBR_G):
            r = NBR_G * grp + ri
            rs = min(max(r - NB_ROWS // 2, 0), ROWS_N - NB_ROWS)
            row = [tiles[start + kr - r + NB_ROWS - 1] if rs <= start + kr < rs + NB_ROWS else masked
                   for kr in range(NBR_KR)]
            o_ref[0, p, ri * GRID_W:(ri + 1) * GRID_W, :] = jnp.concatenate(row, axis=1)


def nbr_bias_table(rpb):
    heads = rpb.shape[0]
    return pl.pallas_call(
        _nbr_bias_body,
        grid=(heads,),
        in_specs=[pl.BlockSpec(memory_space=pltpu.SMEM)],
        out_specs=pl.BlockSpec((1, 3, NBR_G * GRID_W, NBR_KR * GRID_W), lambda h: (h, 0, 0, 0)),
        out_shape=jax.ShapeDtypeStruct((heads, 3, NBR_G * GRID_W, NBR_KR * GRID_W), F32),
        compiler_params=_cp("parallel"),
        name="nbr_bias_table",
    )(rpb.astype(F32).reshape(heads, -1))


def _nbr_attn_body(q_ref, k_ref, v_ref, bias_ref, _, o_ref, *, scale):
    grp = pl.program_id(2)
    n_loc = NBR_KR * GRID_W
    start_row = jnp.clip(_nbr_window_start(grp), 0, ROWS_N - NBR_KR)
    start = pl.multiple_of(start_row * GRID_W, GRID_W)
    q = q_ref[0]
    s_loc = _dot_nt(q, k_ref[0, pl.ds(start, n_loc), :]) * scale + bias_ref[0, 0]
    s_ctx = _dot_nt(q, k_ref[0, S:T, :]) * scale
    o = _softmax_pv([(s_loc, v_ref[0, pl.ds(start, n_loc), :]), (s_ctx, v_ref[0, S:T, :])], None,
                    NBR_G * GRID_W)
    o_ref[0] = o.astype(o_ref.dtype)


def neighbourhood_attention(att, q, k, v, bias_tab, *, heads, col_off, scale):
    tq = NBR_G * GRID_W

    def pattern(grp):
        return jnp.where(grp == 0, 0, jnp.where(grp == NBR_GROUPS - 1, 2, 1))

    return pl.pallas_call(
        functools.partial(_nbr_attn_body, scale=scale),
        grid=(B, heads, NBR_GROUPS),
        in_specs=[
            pl.BlockSpec((1, tq, HD), lambda b, h, r: (b, r, h)),
            pl.BlockSpec((1, T, HD), lambda b, h, r: (b, 0, h)),
            pl.BlockSpec((1, T, HD), lambda b, h, r: (b, 0, h)),
            pl.BlockSpec((1, 1, tq, NBR_KR * GRID_W), lambda b, h, r: (h, pattern(r), 0, 0)),
            pl.BlockSpec(memory_space=pl.ANY),
        ],
        out_specs=pl.BlockSpec((1, tq, HD), lambda b, h, r: (b, r, h + col_off)),
        out_shape=jax.ShapeDtypeStruct(att.shape, att.dtype),
        input_output_aliases={4: 0},
        compiler_params=_cp("parallel", "parallel", "parallel"),
        name="neighbourhood_attention",
    )(q, k, v, bias_tab, att)


def _cumsum_lanes(mask, tri):
    n = mask.shape[1]
    carry = jnp.zeros((mask.shape[0], 1), F32)
    out = []
    for j in range(n // 256):
        blk = jnp.where(mask[:, j * 256:(j + 1) * 256], 1.0, 0.0).astype(BF16)
        c = _dot(blk, tri) + carry
        carry = c[:, 255:256]
        out.append(c)
    return jnp.concatenate(out, axis=1) if len(out) > 1 else out[0]


def _count(mask):
    return jnp.sum(jnp.where(mask, 1.0, 0.0), axis=1, keepdims=True)


def _topk_slots(aff, cap, tri):
    rows = aff.shape[0]

    def step(carry):
        lo, hi, _ = carry
        mid = 0.5 * (lo + hi)
        ok = _count(aff >= mid) >= cap
        nlo = jnp.where(ok, mid, lo)
        nhi = jnp.where(ok, hi, mid)
        moved = jnp.max(jnp.where((nlo != lo) | (nhi != hi), 1.0, 0.0))
        return nlo, nhi, moved

    lo, hi, _ = lax.while_loop(lambda c: c[2] > 0.0, step,
                               (jnp.zeros((rows, 1), F32), jnp.full((rows, 1), 2.0, F32), jnp.float32(1.0)))
    gt = aff >= hi
    eq = (aff >= lo) & (aff < hi)
    need = cap - _count(gt)
    sel = gt | (eq & (_cumsum_lanes(eq, tri) <= need))
    pos = _cumsum_lanes(sel, tri) - 1.0
    return jnp.where(sel, pos.astype(I32), -1)


def _router_body(x_ref, gain_ref, sc_ref, sh_ref, wr_ref, h_ref, slot_ref, aff_ref, lg_scr):
    b = pl.program_id(0)
    tt = pl.program_id(1)
    row = jnp.where(tt == LAT_TILES, CTX_ROW, b)
    y = _rms(x_ref[...], gain_ref[...])
    h = y * (1.0 + sc_ref[0, pl.ds(row, 1), :]) + sh_ref[0, pl.ds(row, 1), :]
    h_ref[...] = h.astype(BF16)
    lg_scr[tt] = lax.dot_general(wr_ref[...], h, (((1,), (1,)), ((), ())),
                                 precision=lax.Precision.HIGHEST, preferred_element_type=F32)

    @pl.when(tt == TILES - 1)
    def _():
        tri = jnp.where(lax.broadcasted_iota(I32, (256, 256), 0) <= lax.broadcasted_iota(I32, (256, 256), 1),
                        1.0, 0.0).astype(BF16)
        lat = jnp.concatenate([lg_scr[j] for j in range(LAT_TILES)], axis=1)
        for lg, cap, off, n in ((lat, CAP_L, 0, S), (lg_scr[LAT_TILES], CAP_C, S, L)):
            ex = jnp.exp(lg - lg.max(axis=0, keepdims=True))
            aff = ex / ex.sum(axis=0, keepdims=True)
            slot_ref[0, :, off:off + n] = _topk_slots(aff, cap, tri)
            aff_ref[0, :, off:off + n] = aff


def router(x2d, gain, modt, l, w_router_t):
    return pl.pallas_call(
        _router_body,
        grid=(B, TILES),
        in_specs=[
            pl.BlockSpec((TM, D), lambda b, t: (b * TILES + t, 0)),
            pl.BlockSpec((1, D), lambda b, t: (0, 0)),
            pl.BlockSpec((1, 8, D), lambda b, t: (l, 0, 4)),
            pl.BlockSpec((1, 8, D), lambda b, t: (l, 0, 3)),
            pl.BlockSpec((E, D), lambda b, t: (0, 0)),
        ],
        out_specs=[
            pl.BlockSpec((TM, D), lambda b, t: (b * TILES + t, 0)),
            pl.BlockSpec((1, E, T), lambda b, t: (b, 0, 0)),
            pl.BlockSpec((1, E, T), lambda b, t: (b, 0, 0)),
        ],
        out_shape=[
            jax.ShapeDtypeStruct((R, D), BF16),
            jax.ShapeDtypeStruct((B, E, T), I32),
            jax.ShapeDtypeStruct((B, E, T), F32),
        ],
        scratch_shapes=[pltpu.VMEM((TILES, E, TM), F32)],
        compiler_params=_cp("parallel", "arbitrary"),
        name="router",
    )(x2d, gain.reshape(1, D), modt, modt, w_router_t)


_SETS = ((0, S, CAP_L, 0), (S, L, CAP_C, CAP_L))


def _dispatch_body(h_ref, slot_ref, aff_ref, xin_ref, gate_ref):
    e = pl.program_id(1)
    srow = slot_ref[0, pl.ds(e, 1), :]
    arow = aff_ref[0, pl.ds(e, 1), :]
    for off, n, cap, o0 in _SETS:
        hit = lax.broadcasted_iota(I32, (cap, n), 0) == srow[:, off:off + n]
        onehot = jnp.where(hit, 1.0, 0.0).astype(BF16)
        xin_ref[0, 0, o0:o0 + cap, :] = _dot(onehot, h_ref[0, off:off + n, :]).astype(BF16)
        gate_ref[0, 0, o0:o0 + cap, :] = jnp.sum(jnp.where(hit, arow[:, off:off + n], 0.0), axis=1, keepdims=True)


def dispatch(h2, slot, aff):
    return pl.pallas_call(
        _dispatch_body,
        grid=(B, E),
        in_specs=[
            pl.BlockSpec((1, T, D), lambda b, e: (b, 0, 0)),
            pl.BlockSpec((1, E, T), lambda b, e: (b, 0, 0)),
            pl.BlockSpec((1, E, T), lambda b, e: (b, 0, 0)),
        ],
        out_specs=[
            pl.BlockSpec((1, 1, SLOTS, D), lambda b, e: (e, b, 0, 0)),
            pl.BlockSpec((1, 1, SLOTS, 1), lambda b, e: (e, b, 0, 0)),
        ],
        out_shape=[
            jax.ShapeDtypeStruct((E, B, SLOTS, D), BF16),
            jax.ShapeDtypeStruct((E, B, SLOTS, 1), F32),
        ],
        compiler_params=_cp("parallel", "parallel"),
        name="moe_dispatch",
    )(h2.reshape(B, T, D), slot, aff)


def _ffn_body(x_ref, wg_ref, wu_ref, wd_ref, gate_ref, o_ref, acc_ref):
    f = pl.program_id(1)
    wg = wg_ref[0].astype(BF16)
    wu = wu_ref[0].astype(BF16)
    wd = wd_ref[0].astype(BF16)

    @pl.when(f == 0)
    def _():
        acc_ref[...] = jnp.zeros_like(acc_ref)

    def chunk(r, carry):
        rows = pl.ds(pl.multiple_of(r * FFN_ROWS, FFN_ROWS), FFN_ROWS)
        x = x_ref[0, rows, :]
        a = _dot(x, wg)
        u = _dot(x, wu)
        hid = (a * jax.nn.sigmoid(a) * u).astype(BF16)
        acc_ref[rows, :] += _dot(hid, wd)
        return carry

    lax.fori_loop(0, x_ref.shape[1] // FFN_ROWS, chunk, 0)

    @pl.when(f == pl.num_programs(1) - 1)
    def _():
        o_ref[0] = (acc_ref[...] * gate_ref[0]).astype(o_ref.dtype)


def expert_ffn(xin, gates, w_gate, w_up, w_down):
    tf = 256
    m = B * SLOTS
    return pl.pallas_call(
        _ffn_body,
        grid=(E, FE // tf),
        in_specs=[
            pl.BlockSpec((1, m, D), lambda e, f: (e, 0, 0)),
            pl.BlockSpec((1, D, tf), lambda e, f: (e, 0, f)),
            pl.BlockSpec((1, D, tf), lambda e, f: (e, 0, f)),
            pl.BlockSpec((1, tf, D), lambda e, f: (e, f, 0)),
            pl.BlockSpec((1, m, 1), lambda e, f: (e, 0, 0)),
        ],
        out_specs=pl.BlockSpec((1, m, D), lambda e, f: (e, 0, 0)),
        out_shape=jax.ShapeDtypeStruct((E, m, D), BF16),
        scratch_shapes=[pltpu.VMEM((m, D), F32)],
        compiler_params=_cp("parallel", "arbitrary"),
        name="expert_ffn",
    )(xin.reshape(E, m, D), w_gate, w_up, w_down, gates.reshape(E, m, 1))


def _combine_body(o_ref, slot_ref, x_ref, g_ref, out_ref):
    b = pl.program_id(0)
    tt = pl.program_id(2)
    tn = out_ref.shape[1]

    def run(cap, o0, row):
        acc = jnp.zeros((TM, tn), F32)
        sl = lax.broadcasted_iota(I32, (TM, cap), 1)
        for e in range(E):
            onehot = jnp.where(slot_ref[0, :, e:e + 1] == sl, 1.0, 0.0).astype(BF16)
            acc = acc + _dot(onehot, o_ref[e, 0, o0:o0 + cap, :])
        out_ref[...] = x_ref[...] + g_ref[0, pl.ds(row, 1), :] * acc

    @pl.when(tt < LAT_TILES)
    def _():
        run(CAP_L, 0, b)

    @pl.when(tt == LAT_TILES)
    def _():
        run(CAP_C, CAP_L, CTX_ROW)


def combine(o, slot_tok, x2d, modt, l):
    tn = 1024
    per = D // tn
    return pl.pallas_call(
        _combine_body,
        grid=(B, per, TILES),
        in_specs=[
            pl.BlockSpec((E, 1, SLOTS, tn), lambda b, j, t: (0, b, 0, j)),
            pl.BlockSpec((1, TM, E), lambda b, j, t: (b, t, 0)),
            pl.BlockSpec((TM, tn), lambda b, j, t: (b * TILES + t, j)),
            pl.BlockSpec((1, 8, tn), lambda b, j, t: (l, 0, 5 * per + j)),
        ],
        out_specs=pl.BlockSpec((TM, tn), lambda b, j, t: (b * TILES + t, j)),
        out_shape=jax.ShapeDtypeStruct((R, D), F32),
        compiler_params=_cp("parallel", "parallel", "parallel"),
        name="moe_combine",
    )(o.reshape(E, B, SLOTS, D), slot_tok, x2d, modt)


def moe_block(x2d, gain, modt, l, w_router_t, w_gate, w_up, w_down):
    h2, slot, aff = router(x2d, gain, modt, l, w_router_t)
    xin, gates = dispatch(h2, slot, aff)
    o = expert_ffn(xin, gates, w_gate, w_up, w_down)
    return combine(o, jnp.swapaxes(slot, 1, 2), x2d, modt, l)


def _rope_tables(d):
    half = d // 2
    t = jnp.arange(S)
    inv = ROPE_BASE ** (-jnp.arange(0, half, 2, dtype=F32) / half)
    parts_c, parts_s = [], []
    for pos in (t // GRID_W, t % GRID_W):
        ang = pos.astype(F32)[:, None] * inv[None, :]
        parts_c += [jnp.cos(ang), jnp.cos(ang)]
        parts_s += [-jnp.sin(ang), jnp.sin(ang)]
    cos = jnp.concatenate([jnp.concatenate(parts_c, axis=1), jnp.ones((L, d), F32)], axis=0)
    sin = jnp.concatenate([jnp.concatenate(parts_s, axis=1), jnp.zeros((L, d), F32)], axis=0)
    return cos, sin


def _rope_lanes(x, cos, sin, quarter):
    lane = lax.broadcasted_iota(I32, x.shape, 1)
    sw = jnp.where(lane % (2 * quarter) < quarter, pltpu.roll(x, HD - quarter, 1), pltpu.roll(x, quarter, 1))
    return x * cos + sw * sin


def _table_spec():
    return pl.BlockSpec((TM, HD), lambda i: (i % TILES, 0))


def _post_ab_body(p_ref, c128_ref, s128_ref, c64_ref, s64_ref, qn_ref, kvn_ref,
                  aq_ref, ak_ref, av_ref, cq_ref, ckv_ref, kr_ref):
    cos, sin = c128_ref[...], s128_ref[...]
    for h in range(8):
        aq_ref[:, h * HD:(h + 1) * HD] = _rope_lanes(p_ref[:, h * HD:(h + 1) * HD], cos, sin, 32).astype(BF16)
    for h in range(2):
        ak_ref[:, h * HD:(h + 1) * HD] = _rope_lanes(p_ref[:, 1024 + h * HD:1024 + (h + 1) * HD], cos, sin,
                                                     32).astype(BF16)
    av_ref[...] = p_ref[:, 1280:1536].astype(BF16)
    cq_ref[...] = _rms(p_ref[:, 1536:2048], qn_ref[...]).astype(BF16)
    ckv_ref[...] = _rms(p_ref[:, 2048:2304], kvn_ref[...]).astype(BF16)
    kr = jnp.concatenate([p_ref[:, 2304:2368], jnp.zeros((TM, 64), F32)], axis=1)
    kr_ref[...] = _rope_lanes(kr, c64_ref[...], s64_ref[...], 16).astype(BF16)


def post_project_ab(proj, tabs128, tabs64pad, q_norm, kv_norm):
    widths = (1024, 256, 256, MLA_Q, MLA_KV, HD)
    return pl.pallas_call(
        _post_ab_body,
        grid=(R // TM,),
        in_specs=[pl.BlockSpec((TM, AB_IN), lambda i: (i, 0))] + [_table_spec()] * 4
        + [pl.BlockSpec((1, MLA_Q), lambda i: (0, 0)), pl.BlockSpec((1, MLA_KV), lambda i: (0, 0))],
        out_specs=[pl.BlockSpec((TM, w), lambda i: (i, 0)) for w in widths],
        out_shape=[jax.ShapeDtypeStruct((R, w), BF16) for w in widths],
        compiler_params=_cp("parallel"),
        name="post_project_ab",
    )(proj, *tabs128, *tabs64pad, q_norm.reshape(1, MLA_Q), kv_norm.reshape(1, MLA_KV))


def _q_up_body(cq_ref, w_ref, c64_ref, s64_ref, o_ref):
    y = _dot(cq_ref[...], w_ref[...])
    cos, sin = c64_ref[...], s64_ref[...]
    zeros = jnp.zeros((TM, 64), F32)
    for j in range(4):
        rot = _rope_lanes(y[:, 1024 + j * HD:1024 + (j + 1) * HD], cos, sin, 16)
        for u in range(2):
            h = 2 * j + u
            o_ref[:, 256 * h:256 * h + HD] = y[:, h * HD:(h + 1) * HD].astype(BF16)
            o_ref[:, 256 * h + HD:256 * (h + 1)] = jnp.concatenate([rot[:, 64 * u:64 * (u + 1)], zeros],
                                                                   axis=1).astype(BF16)


def mla_q_up(cq, w_uq_split, tabs64x2):
    return pl.pallas_call(
        _q_up_body,
        grid=(R // TM,),
        in_specs=[pl.BlockSpec((TM, MLA_Q), lambda i: (i, 0)), pl.BlockSpec((MLA_Q, 1536), lambda i: (0, 0)),
                  _table_spec(), _table_spec()],
        out_specs=pl.BlockSpec((TM, 2048), lambda i: (i, 0)),
        out_shape=jax.ShapeDtypeStruct((R, 2048), BF16),
        compiler_params=_cp("parallel"),
        name="mla_q_up",
    )(cq, w_uq_split, *tabs64x2)


def _kv_up_body(ckv_ref, w_ref, kr_ref, k_ref, v_ref):
    y = _dot(ckv_ref[...], w_ref[...])
    kr = kr_ref[...]
    for h in range(8):
        k_ref[:, 256 * h:256 * h + HD] = y[:, h * HD:(h + 1) * HD].astype(BF16)
        k_ref[:, 256 * h + HD:256 * (h + 1)] = kr
    v_ref[...] = y[:, 1024:].astype(BF16)


def mla_kv_up(ckv, w_ukv_split, kr):
    return pl.pallas_call(
        _kv_up_body,
        grid=(R // TM,),
        in_specs=[pl.BlockSpec((TM, MLA_KV), lambda i: (i, 0)), pl.BlockSpec((MLA_KV, 2048), lambda i: (0, 0)),
                  pl.BlockSpec((TM, HD), lambda i: (i, 0))],
        out_specs=[pl.BlockSpec((TM, 2048), lambda i: (i, 0)), pl.BlockSpec((TM, 1024), lambda i: (i, 0))],
        out_shape=[jax.ShapeDtypeStruct((R, 2048), BF16), jax.ShapeDtypeStruct((R, 1024), BF16)],
        compiler_params=_cp("parallel"),
        name="mla_kv_up",
    )(ckv, w_ukv_split, kr)


def _post_cd_body(p_ref, c128_ref, s128_ref, qn_ref, kn_ref, cq_ref, ck_ref, cv_ref, dq_ref, dk_ref, dv_ref):
    cos, sin = c128_ref[...], s128_ref[...]
    for h in range(8):
        x = _rms(p_ref[:, h * HD:(h + 1) * HD], qn_ref[...])
        cq_ref[:, h * HD:(h + 1) * HD] = _rope_lanes(x, cos, sin, 32).astype(BF16)
    for h in range(2):
        x = _rms(p_ref[:, 1024 + h * HD:1024 + (h + 1) * HD], kn_ref[...])
        ck_ref[:, h * HD:(h + 1) * HD] = _rope_lanes(x, cos, sin, 32).astype(BF16)
    cv_ref[...] = p_ref[:, 1280:1536].astype(BF16)
    dq_ref[...] = p_ref[:, 1536:2560].astype(BF16)
    dk_ref[...] = p_ref[:, 2560:3584].astype(BF16)
    dv_ref[...] = p_ref[:, 3584:4608].astype(BF16)


def post_project_cd(proj, tabs128, q_norm, k_norm):
    widths = (1024, 256, 256, 1024, 1024, 1024)
    return pl.pallas_call(
        _post_cd_body,
        grid=(R // TM,),
        in_specs=[pl.BlockSpec((TM, CD_IN), lambda i: (i, 0)), _table_spec(), _table_spec(),
                  pl.BlockSpec((1, HD), lambda i: (0, 0)), pl.BlockSpec((1, HD), lambda i: (0, 0))],
        out_specs=[pl.BlockSpec((TM, w), lambda i: (i, 0)) for w in widths],
        out_shape=[jax.ShapeDtypeStruct((R, w), BF16) for w in widths],
        compiler_params=_cp("parallel"),
        name="post_project_cd",
    )(proj, *tabs128, q_norm.reshape(1, HD), k_norm.reshape(1, HD))


def _sink_col(sink, kvh, g, tq):
    return jnp.broadcast_to(sink.astype(F32).reshape(kvh, g, 1, 1), (kvh, g, tq, 1)).reshape(kvh, g * tq, 1)


def _split_heads(w, n_first):
    k = w.shape[0]
    w3 = w.reshape(k, 8, -1)
    return jnp.concatenate([w3[:, :, :n_first].reshape(k, -1), w3[:, :, n_first:].reshape(k, -1)], axis=1)


def mixer_ab(h1, need_ctx, w_in, sink, q_norm, w_uq, kv_norm, w_ukv, tabs):
    proj = matmul(h1, w_in.astype(BF16), F32, 512, AB_IN)
    aq, ak, av, cq, ckv, kr = post_project_ab(proj, tabs["r128"], tabs["r64pad"], q_norm, kv_norm)
    bq = mla_q_up(cq, _split_heads(w_uq, 128).astype(BF16), tabs["r64x2"])
    bk, bv = mla_kv_up(ckv, _split_heads(w_ukv, 128).astype(BF16), kr)
    aq, ak, av, bq, bk, bv = (t.reshape(B, T, -1) for t in (aq, ak, av, bq, bk, bv))

    scale_a = HD ** -0.5
    scale_b = 192 ** -0.5
    ctx_blk = S // L
    att = jnp.zeros((B, T, D), BF16)
    att = window_attention(att, aq, ak, av, _sink_col(sink, 2, 4, BQ), kvh=2, g=4, scale=scale_a)
    att = dense_attention(att, bq, bk, bv, kvh=8, g=1, dk=256, dv=HD, tq=512, q_off=0, n_q=S // 512, nk=T,
                          k_blk=0, col_off=8, scale=scale_b)
    if need_ctx:
        att = dense_attention(att, aq, ak, av, kvh=2, g=4, dk=HD, dv=HD, tq=L, q_off=ctx_blk, n_q=1, nk=L,
                              k_blk=ctx_blk, col_off=0, scale=scale_a, sink_col=_sink_col(sink, 2, 4, L))
        att = dense_attention(att, bq, bk, bv, kvh=8, g=1, dk=256, dv=HD, tq=L, q_off=ctx_blk, n_q=1, nk=L,
                              k_blk=ctx_blk, col_off=8, scale=scale_b)
    return att.reshape(R, D)


def mixer_cd(h1, need_ctx, w_in, q_norm, k_norm, rpb, tabs):
    proj = matmul(h1, w_in.astype(BF16), F32, 512, CD_IN // 2)
    cq, ck, cv, dq, dk, dv = (t.reshape(B, T, -1)
                              for t in post_project_cd(proj, tabs["r128"], q_norm, k_norm))
    scale = HD ** -0.5
    ctx_blk = S // L
    att = jnp.zeros((B, T, D), BF16)
    att = dense_attention(att, cq, ck, cv, kvh=2, g=4, dk=HD, dv=HD, tq=BQ, q_off=0, n_q=S // BQ, nk=T, k_blk=0,
                          col_off=0, scale=scale)
    att = neighbourhood_attention(att, dq, dk, dv, nbr_bias_table(rpb), heads=8, col_off=8, scale=scale)
    if need_ctx:
        att = dense_attention(att, cq, ck, cv, kvh=2, g=4, dk=HD, dv=HD, tq=L, q_off=ctx_blk, n_q=1, nk=L,
                              k_blk=ctx_blk, col_off=0, scale=scale)
        att = dense_attention(att, dq, dk, dv, kvh=8, g=1, dk=HD, dv=HD, tq=L, q_off=ctx_blk, n_q=1, nk=L,
                              k_blk=ctx_blk, col_off=8, scale=scale)
    return att.reshape(R, D)


def _final_norm_body(x_ref, g_ref, o_ref):
    o_ref[0] = _rms(x_ref[0], g_ref[...])


def final_rmsnorm(x3, gain):
    return pl.pallas_call(
        _final_norm_body,
        grid=(B, S // TM),
        in_specs=[pl.BlockSpec((1, TM, D), lambda b, i: (b, i, 0)), pl.BlockSpec((1, D), lambda b, i: (0, 0))],
        out_specs=pl.BlockSpec((1, TM, D), lambda b, i: (b, i, 0)),
        out_shape=jax.ShapeDtypeStruct((B, S, D), F32),
        compiler_params=_cp("parallel", "parallel"),
        name="final_norm",
    )(x3, gain.reshape(1, D))


def kernel(x, c, ctx, c_ctx, w_ada, b_ada, norm1, norm2, ab_w_in, ab_sink, ab_q_norm, ab_w_uq, ab_kv_norm,
           ab_w_ukv, ab_w_out, cd_w_in, cd_q_norm, cd_k_norm, cd_rpb, cd_w_out, w_router, w_gate, w_up, w_down,
           final_norm):
    c8 = jnp.concatenate([c, c_ctx[None, :], jnp.zeros((8 - B - 1, D), F32)], axis=0)
    modt = ada_modulation(c8, w_ada, b_ada)
    c64, s64 = _rope_tables(64)
    zeros64 = jnp.zeros((T, 64), F32)
    tabs = {
        "r128": _rope_tables(HD),
        "r64x2": (jnp.concatenate([c64, c64], axis=1), jnp.concatenate([s64, s64], axis=1)),
        "r64pad": (jnp.concatenate([c64, zeros64], axis=1), jnp.concatenate([s64, zeros64], axis=1)),
    }
    x2d = jnp.concatenate([x, ctx], axis=1).reshape(R, D)
    for l in range(DEPTH):
        need_ctx = l < DEPTH - 1
        i = l // 2
        h1 = norm_modulate(x2d, norm1[l], modt, l, 1, 0)
        if l % 2 == 0:
            att = mixer_ab(h1, need_ctx, ab_w_in[i], ab_sink[i], ab_q_norm[i], ab_w_uq[i], ab_kv_norm[i],
                           ab_w_ukv[i], tabs)
            w_out = ab_w_out[i]
        else:
            att = mixer_cd(h1, need_ctx, cd_w_in[i], cd_q_norm[i], cd_k_norm[i], cd_rpb[i], tabs)
            w_out = cd_w_out[i]
        x2d = matmul_gated_residual(att, w_out.astype(BF16), x2d, modt, l, 2)
        x2d = moe_block(x2d, norm2[l], modt, l, w_router[l].T, w_gate[l], w_up[l], w_down[l])
    return final_rmsnorm(x2d.reshape(B, T, D), final_norm)
```

```python
import functools

import jax
import jax.numpy as jnp
from jax import lax
from jax.experimental import pallas as pl
from jax.experimental.pallas import tpu as pltpu

F32 = jnp.float32
BF16 = jnp.bfloat16
I32 = jnp.int32

D = 2048
B = 4
S = 2048
L = 256
T = S + L
R = B * T
DEPTH = 4
GRID_W = 64
HD = 128
ROPE_BASE = 10000.0
EPS = 1e-6
NEG = -1e30
WINDOW = 128
BQ = 128
NB_ROWS = 8
NB_COLS = 16
E = 16
CAP_L = 2 * S // E
CAP_C = 2 * L // E
SLOTS = CAP_L + CAP_C
FE = 1024
FFN_ROWS = 384
MLA_Q = 512
MLA_KV = 256
AB_IN = 2368
CD_IN = 4608

ATTN_SUB = 128
TM = 256
TILES = T // TM
LAT_TILES = S // TM
CTX_ROW = B
VMEM_LIMIT = 56 * 1024 * 1024


def _cp(*sem):
    return pltpu.CompilerParams(dimension_semantics=sem, vmem_limit_bytes=VMEM_LIMIT)


def _mod_row(tile):
    b = tile // TILES
    tt = tile % TILES
    return jnp.where(tt == LAT_TILES, CTX_ROW, b)


def _dot(a, b):
    return jnp.dot(a, b, preferred_element_type=F32)


def _dot_nt(a, b):
    return lax.dot_general(a, b, (((1,), (1,)), ((), ())), preferred_element_type=F32)


def _ada_body(c_ref, w_ref, b_ref, o_ref):
    c = c_ref[...]
    a = c * jax.nn.sigmoid(c)
    o_ref[0] = _dot(a.astype(BF16), w_ref[0].astype(BF16)) + b_ref[0]


def ada_modulation(c8, w_ada, b_ada):
    tn = 1024
    n = 6 * D
    return pl.pallas_call(
        _ada_body,
        grid=(DEPTH, n // tn),
        in_specs=[
            pl.BlockSpec((8, D), lambda l, j: (0, 0)),
            pl.BlockSpec((1, D, tn), lambda l, j: (l, 0, j)),
            pl.BlockSpec((1, 1, tn), lambda l, j: (l, 0, j)),
        ],
        out_specs=pl.BlockSpec((1, 8, tn), lambda l, j: (l, 0, j)),
        out_shape=jax.ShapeDtypeStruct((DEPTH, 8, n), F32),
        compiler_params=_cp("parallel", "parallel"),
        name="ada_modulation",
    )(c8, w_ada, b_ada.reshape(DEPTH, 1, n))


def _rms(x, gain):
    return x * lax.rsqrt(jnp.mean(x * x, axis=-1, keepdims=True) + EPS) * gain


def _normmod_body(x_ref, gain_ref, sc_ref, sh_ref, o_ref):
    row = _mod_row(pl.program_id(0))
    y = _rms(x_ref[...], gain_ref[...])
    o_ref[...] = (y * (1.0 + sc_ref[0, pl.ds(row, 1), :]) + sh_ref[0, pl.ds(row, 1), :]).astype(o_ref.dtype)


def norm_modulate(x2d, gain, modt, l, chunk_sc, chunk_sh):
    return pl.pallas_call(
        _normmod_body,
        grid=(R // TM,),
        in_specs=[
            pl.BlockSpec((TM, D), lambda i: (i, 0)),
            pl.BlockSpec((1, D), lambda i: (0, 0)),
            pl.BlockSpec((1, 8, D), lambda i: (l, 0, chunk_sc)),
            pl.BlockSpec((1, 8, D), lambda i: (l, 0, chunk_sh)),
        ],
        out_specs=pl.BlockSpec((TM, D), lambda i: (i, 0)),
        out_shape=jax.ShapeDtypeStruct((R, D), BF16),
        compiler_params=_cp("parallel"),
        name="norm_modulate",
    )(x2d, gain.reshape(1, D), modt, modt)


def _mm_body(a_ref, b_ref, o_ref):
    o_ref[...] = _dot(a_ref[...].astype(BF16), b_ref[...].astype(BF16)).astype(o_ref.dtype)


def matmul(a, b, out_dtype, tm, tn):
    m, k = a.shape
    n = b.shape[1]
    return pl.pallas_call(
        _mm_body,
        grid=(n // tn, m // tm),
        in_specs=[
            pl.BlockSpec((tm, k), lambda j, i: (i, 0)),
            pl.BlockSpec((k, tn), lambda j, i: (0, j)),
        ],
        out_specs=pl.BlockSpec((tm, tn), lambda j, i: (i, j)),
        out_shape=jax.ShapeDtypeStruct((m, n), out_dtype),
        compiler_params=_cp("parallel", "parallel"),
        name="matmul",
    )(a, b)


def _mm_res_body(a_ref, b_ref, x_ref, g_ref, o_ref):
    row = _mod_row(pl.program_id(1))
    y = _dot(a_ref[...], b_ref[...])
    o_ref[...] = x_ref[...] + g_ref[0, pl.ds(row, 1), :] * y


def matmul_gated_residual(a, b, x2d, modt, l, chunk):
    tn = 1024
    k = a.shape[1]
    per = D // tn
    return pl.pallas_call(
        _mm_res_body,
        grid=(D // tn, R // TM),
        in_specs=[
            pl.BlockSpec((TM, k), lambda j, i: (i, 0)),
            pl.BlockSpec((k, tn), lambda j, i: (0, j)),
            pl.BlockSpec((TM, tn), lambda j, i: (i, j)),
            pl.BlockSpec((1, 8, tn), lambda j, i: (l, 0, chunk * per + j)),
        ],
        out_specs=pl.BlockSpec((TM, tn), lambda j, i: (i, j)),
        out_shape=jax.ShapeDtypeStruct((R, D), F32),
        compiler_params=_cp("parallel", "parallel"),
        name="matmul_gated_residual",
    )(a, b, x2d, modt)


def _softmax_pv(parts, sink, rows):
    m = parts[0][0].max(axis=-1, keepdims=True)
    for s, _ in parts[1:]:
        m = jnp.maximum(m, s.max(axis=-1, keepdims=True))
    if sink is not None:
        m = jnp.maximum(m, sink)
    den = jnp.zeros((rows, 1), F32) if sink is None else jnp.exp(sink - m)
    acc = None
    for s, v in parts:
        p = jnp.exp(s - m)
        den = den + p.sum(axis=-1, keepdims=True)
        pv = _dot(p.astype(BF16), v)
        acc = pv if acc is None else acc + pv
    return acc / den


def _dense_attn_body(*refs, g, tq, dk, dv, scale, has_sink):
    if has_sink:
        q_ref, k_ref, v_ref, sink_ref, _, o_ref = refs
        sink = sink_ref[0]
    else:
        q_ref, k_ref, v_ref, _, o_ref = refs
        sink = None
    sub = min(tq, ATTN_SUB)
    for j in range(g):
        for r in range(tq // sub):
            rows = slice(r * sub, (r + 1) * sub)
            s = _dot_nt(q_ref[0, rows, j * dk:(j + 1) * dk], k_ref[0]) * scale
            sk = None if sink is None else sink[j * tq + r * sub:j * tq + (r + 1) * sub]
            o = _softmax_pv([(s, v_ref[0])], sk, sub)
            o_ref[0, rows, j * dv:(j + 1) * dv] = o.astype(o_ref.dtype)


def dense_attention(att, q, k, v, *, kvh, g, dk, dv, tq, q_off, n_q, nk, k_blk, col_off, scale, sink_col=None):
    in_specs = [
        pl.BlockSpec((1, tq, g * dk), lambda b, h, i: (b, i + q_off, h)),
        pl.BlockSpec((1, nk, dk), lambda b, h, i: (b, k_blk, h)),
        pl.BlockSpec((1, nk, dv), lambda b, h, i: (b, k_blk, h)),
    ]
    args = [q, k, v]
    if sink_col is not None:
        in_specs.append(pl.BlockSpec((1, g * tq, 1), lambda b, h, i: (h, 0, 0)))
        args.append(sink_col)
    in_specs.append(pl.BlockSpec(memory_space=pl.ANY))
    args.append(att)
    return pl.pallas_call(
        functools.partial(_dense_attn_body, g=g, tq=tq, dk=dk, dv=dv, scale=scale,
                          has_sink=sink_col is not None),
        grid=(B, kvh, n_q),
        in_specs=in_specs,
        out_specs=pl.BlockSpec((1, tq, g * dv), lambda b, h, i: (b, i + q_off, h + col_off)),
        out_shape=jax.ShapeDtypeStruct(att.shape, att.dtype),
        input_output_aliases={len(args) - 1: 0},
        compiler_params=_cp("parallel", "parallel", "parallel"),
        name="dense_attention",
    )(*args)


def _window_attn_body(q_ref, kp_ref, kc_ref, kn_ref, kx_ref, vp_ref, vc_ref, vn_ref, vx_ref, sink_ref,
                      _, o_ref, *, g, scale):
    n = pl.program_id(2)
    nb = S // BQ
    kl = jnp.concatenate([kp_ref[0], kc_ref[0], kn_ref[0]], axis=0)
    vl = jnp.concatenate([vp_ref[0], vc_ref[0], vn_ref[0]], axis=0)
    qi = lax.broadcasted_iota(I32, (BQ, 3 * BQ), 0) + BQ
    kj = lax.broadcasted_iota(I32, (BQ, 3 * BQ), 1)
    lo = jnp.where(n > 0, 0, BQ)
    hi = jnp.where(n < nb - 1, 3 * BQ, 2 * BQ)
    valid = (jnp.abs(qi - kj) <= WINDOW) & (kj >= lo) & (kj < hi)
    for j in range(g):
        q = q_ref[0, :, j * HD:(j + 1) * HD]
        s_loc = jnp.where(valid, _dot_nt(q, kl) * scale, NEG)
        s_ctx = _dot_nt(q, kx_ref[0]) * scale
        o = _softmax_pv([(s_loc, vl), (s_ctx, vx_ref[0])], sink_ref[0, j * BQ:(j + 1) * BQ], BQ)
        o_ref[0, :, j * HD:(j + 1) * HD] = o.astype(o_ref.dtype)


def window_attention(att, q, k, v, sink_col, *, kvh, g, scale):
    nb = S // BQ
    ctx_blk = S // L

    def kv_specs():
        return [
            pl.BlockSpec((1, BQ, HD), lambda b, h, n: (b, jnp.maximum(n - 1, 0), h)),
            pl.BlockSpec((1, BQ, HD), lambda b, h, n: (b, n, h)),
            pl.BlockSpec((1, BQ, HD), lambda b, h, n: (b, jnp.minimum(n + 1, nb - 1), h)),
            pl.BlockSpec((1, L, HD), lambda b, h, n: (b, ctx_blk, h)),
        ]

    return pl.pallas_call(
        functools.partial(_window_attn_body, g=g, scale=scale),
        grid=(B, kvh, nb),
        in_specs=[pl.BlockSpec((1, BQ, g * HD), lambda b, h, n: (b, n, h))] + kv_specs() + kv_specs()
        + [pl.BlockSpec((1, g * BQ, 1), lambda b, h, n: (h, 0, 0)), pl.BlockSpec(memory_space=pl.ANY)],
        out_specs=pl.BlockSpec((1, BQ, g * HD), lambda b, h, n: (b, n, h)),
        out_shape=jax.ShapeDtypeStruct(att.shape, att.dtype),
        input_output_aliases={10: 0},
        compiler_params=_cp("parallel", "parallel", "parallel"),
        name="window_attention",
    )(q, k, k, k, k, v, v, v, v, sink_col, att)


ROWS_N = S // GRID_W
NBR_G = 4
NBR_KR = NBR_G + NB_ROWS
NBR_GROUPS = ROWS_N // NBR_G
NBR_PATTERN_GROUPS = (0, 1, NBR_GROUPS - 1)


def _nbr_window_start(grp):
    return NBR_G * grp - NB_ROWS // 2


def _nbr_bias_body(rpb_ref, o_ref):
    h = pl.program_id(0)
    n_dc = 2 * NB_COLS - 1
    n_dr = 2 * NB_ROWS - 1
    qc = lax.broadcasted_iota(I32, (GRID_W, GRID_W), 0)
    kc = lax.broadcasted_iota(I32, (GRID_W, GRID_W), 1)
    dc = jnp.clip(kc - qc + NB_COLS - 1, 0, n_dc - 1)
    cs = jnp.clip(qc - NB_COLS // 2, 0, GRID_W - NB_COLS)
    col_ok = (kc >= cs) & (kc < cs + NB_COLS)
    tiles = [jnp.zeros((GRID_W, GRID_W), F32) for _ in range(n_dr)]
    for t in range(n_dc):
        hit = dc == t
        for dr in range(n_dr):
            tiles[dr] = jnp.where(hit, rpb_ref[h, dr * n_dc + t], tiles[dr])
    tiles = [jnp.where(col_ok, x, NEG) for x in tiles]
    masked = jnp.full((GRID_W, GRID_W), NEG, F32)
    for p, grp in enumerate(NBR_PATTERN_GROUPS):
        start = min(max(_nbr_window_start(grp), 0), ROWS_N - NBR_KR)
        for ri in range(NBR_G):
            r = NBR_G * grp + ri
            rs = min(max(r - NB_ROWS // 2, 0), ROWS_N - NB_ROWS)
            row = [tiles[start + kr - r + NB_ROWS - 1] if rs <= start + kr < rs + NB_ROWS else masked
                   for kr in range(NBR_KR)]
            o_ref[0, p, ri * GRID_W:(ri + 1) * GRID_W, :] = jnp.concatenate(row, axis=1)


def nbr_bias_table(rpb):
    heads = rpb.shape[0]
    return pl.pallas_call(
        _nbr_bias_body,
        grid=(heads,),
        in_specs=[pl.BlockSpec(memory_space=pltpu.SMEM)],
        out_specs=pl.BlockSpec((1, 3, NBR_G * GRID_W, NBR_KR * GRID_W), lambda h: (h, 0, 0, 0)),
        out_shape=jax.ShapeDtypeStruct((heads, 3, NBR_G * GRID_W, NBR_KR * GRID_W), F32),
        compiler_params=_cp("parallel"),
        name="nbr_bias_table",
    )(rpb.astype(F32).reshape(heads, -1))


def _nbr_attn_body(q_ref, k_ref, v_ref, bias_ref, _, o_ref, *, scale):
    grp = pl.program_id(2)
    n_loc = NBR_KR * GRID_W
    start_row = jnp.clip(_nbr_window_start(grp), 0, ROWS_N - NBR_KR)
    start = pl.multiple_of(start_row * GRID_W, GRID_W)
    kl = k_ref[0, pl.ds(start, n_loc), :]
    vl = v_ref[0, pl.ds(start, n_loc), :]
    for r in range(NBR_G * GRID_W // ATTN_SUB):
        rows = slice(r * ATTN_SUB, (r + 1) * ATTN_SUB)
        q = q_ref[0, rows, :]
        s_loc = _dot_nt(q, kl) * scale + bias_ref[0, 0, rows, :]
        s_ctx = _dot_nt(q, k_ref[0, S:T, :]) * scale
        o = _softmax_pv([(s_loc, vl), (s_ctx, v_ref[0, S:T, :])], None, ATTN_SUB)
        o_ref[0, rows, :] = o.astype(o_ref.dtype)


def neighbourhood_attention(att, q, k, v, bias_tab, *, heads, col_off, scale):
    tq = NBR_G * GRID_W

    def pattern(grp):
        return jnp.where(grp == 0, 0, jnp.where(grp == NBR_GROUPS - 1, 2, 1))

    return pl.pallas_call(
        functools.partial(_nbr_attn_body, scale=scale),
        grid=(B, heads, NBR_GROUPS),
        in_specs=[
            pl.BlockSpec((1, tq, HD), lambda b, h, r: (b, r, h)),
            pl.BlockSpec((1, T, HD), lambda b, h, r: (b, 0, h)),
            pl.BlockSpec((1, T, HD), lambda b, h, r: (b, 0, h)),
            pl.BlockSpec((1, 1, tq, NBR_KR * GRID_W), lambda b, h, r: (h, pattern(r), 0, 0)),
            pl.BlockSpec(memory_space=pl.ANY),
        ],
        out_specs=pl.BlockSpec((1, tq, HD), lambda b, h, r: (b, r, h + col_off)),
        out_shape=jax.ShapeDtypeStruct(att.shape, att.dtype),
        input_output_aliases={4: 0},
        compiler_params=_cp("parallel", "parallel", "parallel"),
        name="neighbourhood_attention",
    )(q, k, v, bias_tab, att)


def _cumsum_lanes(mask, tri):
    n = mask.shape[1]
    carry = jnp.zeros((mask.shape[0], 1), F32)
    out = []
    for j in range(n // 256):
        blk = jnp.where(mask[:, j * 256:(j + 1) * 256], 1.0, 0.0).astype(BF16)
        c = _dot(blk, tri) + carry
        carry = c[:, 255:256]
        out.append(c)
    return jnp.concatenate(out, axis=1) if len(out) > 1 else out[0]


def _count(mask):
    return jnp.sum(jnp.where(mask, 1.0, 0.0), axis=1, keepdims=True)


def _topk_slots(affs, caps, tri):
    rows = affs[0].shape[0]

    def step(carry):
        bounds, _ = carry
        new, moved = [], []
        for aff, cap, (lo, hi) in zip(affs, caps, bounds):
            mid = 0.5 * (lo + hi)
            ok = _count(aff >= mid) >= cap
            nlo = jnp.where(ok, mid, lo)
            nhi = jnp.where(ok, hi, mid)
            new.append((nlo, nhi))
            moved.append(jnp.where((nlo != lo) | (nhi != hi), 1.0, 0.0))
        return tuple(new), jnp.max(functools.reduce(jnp.maximum, moved))

    start = tuple((jnp.zeros((rows, 1), F32), jnp.full((rows, 1), 2.0, F32)) for _ in affs)
    bounds, _ = lax.while_loop(lambda c: c[1] > 0.0, step, (start, jnp.float32(1.0)))
    slots = []
    for aff, cap, (lo, hi) in zip(affs, caps, bounds):
        gt = aff >= hi
        eq = (aff >= lo) & (aff < hi)
        need = cap - _count(gt)
        sel = gt | (eq & (_cumsum_lanes(eq, tri) <= need))
        pos = _cumsum_lanes(sel, tri) - 1.0
        slots.append(jnp.where(sel, pos.astype(I32), -1))
    return slots


def _router_body(x_ref, gain_ref, sc_ref, sh_ref, wr_ref, h_ref, slot_ref, aff_ref, lg_scr):
    b = pl.program_id(0)
    tt = pl.program_id(1)
    row = jnp.where(tt == LAT_TILES, CTX_ROW, b)
    y = _rms(x_ref[...], gain_ref[...])
    h = y * (1.0 + sc_ref[0, pl.ds(row, 1), :]) + sh_ref[0, pl.ds(row, 1), :]
    h_ref[...] = h.astype(BF16)
    lg_scr[tt] = lax.dot_general(wr_ref[...], h, (((1,), (1,)), ((), ())),
                                 precision=lax.Precision.HIGHEST, preferred_element_type=F32)

    @pl.when(tt == TILES - 1)
    def _():
        tri = jnp.where(lax.broadcasted_iota(I32, (256, 256), 0) <= lax.broadcasted_iota(I32, (256, 256), 1),
                        1.0, 0.0).astype(BF16)
        lat = jnp.concatenate([lg_scr[j] for j in range(LAT_TILES)], axis=1)
        affs = []
        for lg in (lat, lg_scr[LAT_TILES]):
            ex = jnp.exp(lg - lg.max(axis=0, keepdims=True))
            affs.append(ex / ex.sum(axis=0, keepdims=True))
        slots = _topk_slots(affs, (CAP_L, CAP_C), tri)
        for (off, n, _, _), aff, slot in zip(_SETS, affs, slots):
            slot_ref[0, :, off:off + n] = slot
            aff_ref[0, :, off:off + n] = aff


def router(x2d, gain, modt, l, w_router_t):
    return pl.pallas_call(
        _router_body,
        grid=(B, TILES),
        in_specs=[
            pl.BlockSpec((TM, D), lambda b, t: (b * TILES + t, 0)),
            pl.BlockSpec((1, D), lambda b, t: (0, 0)),
            pl.BlockSpec((1, 8, D), lambda b, t: (l, 0, 4)),
            pl.BlockSpec((1, 8, D), lambda b, t: (l, 0, 3)),
            pl.BlockSpec((E, D), lambda b, t: (0, 0)),
        ],
        out_specs=[
            pl.BlockSpec((TM, D), lambda b, t: (b * TILES + t, 0)),
            pl.BlockSpec((1, E, T), lambda b, t: (b, 0, 0)),
            pl.BlockSpec((1, E, T), lambda b, t: (b, 0, 0)),
        ],
        out_shape=[
            jax.ShapeDtypeStruct((R, D), BF16),
            jax.ShapeDtypeStruct((B, E, T), I32),
            jax.ShapeDtypeStruct((B, E, T), F32),
        ],
        scratch_shapes=[pltpu.VMEM((TILES, E, TM), F32)],
        compiler_params=_cp("parallel", "arbitrary"),
        name="router",
    )(x2d, gain.reshape(1, D), modt, modt, w_router_t)


_SETS = ((0, S, CAP_L, 0), (S, L, CAP_C, CAP_L))


def _dispatch_body(h_ref, slot_ref, aff_ref, xin_ref, gate_ref):
    e = pl.program_id(1)
    srow = slot_ref[0, pl.ds(e, 1), :]
    arow = aff_ref[0, pl.ds(e, 1), :]
    for off, n, cap, o0 in _SETS:
        hit = lax.broadcasted_iota(I32, (cap, n), 0) == srow[:, off:off + n]
        onehot = jnp.where(hit, 1.0, 0.0).astype(BF16)
        xin_ref[0, 0, o0:o0 + cap, :] = _dot(onehot, h_ref[0, off:off + n, :]).astype(BF16)
        gate_ref[0, 0, o0:o0 + cap, :] = jnp.sum(jnp.where(hit, arow[:, off:off + n], 0.0), axis=1, keepdims=True)


def dispatch(h2, slot, aff):
    return pl.pallas_call(
        _dispatch_body,
        grid=(B, E),
        in_specs=[
            pl.BlockSpec((1, T, D), lambda b, e: (b, 0, 0)),
            pl.BlockSpec((1, E, T), lambda b, e: (b, 0, 0)),
            pl.BlockSpec((1, E, T), lambda b, e: (b, 0, 0)),
        ],
        out_specs=[
            pl.BlockSpec((1, 1, SLOTS, D), lambda b, e: (e, b, 0, 0)),
            pl.BlockSpec((1, 1, SLOTS, 1), lambda b, e: (e, b, 0, 0)),
        ],
        out_shape=[
            jax.ShapeDtypeStruct((E, B, SLOTS, D), BF16),
            jax.ShapeDtypeStruct((E, B, SLOTS, 1), F32),
        ],
        compiler_params=_cp("parallel", "parallel"),
        name="moe_dispatch",
    )(h2.reshape(B, T, D), slot, aff)


def _ffn_body(x_ref, wg_ref, wu_ref, wd_ref, gate_ref, o_ref, acc_ref):
    f = pl.program_id(1)
    wg = wg_ref[0, 0].astype(BF16)
    wu = wu_ref[0, 0].astype(BF16)
    wd = wd_ref[0, 0].astype(BF16)

    @pl.when(f == 0)
    def _():
        acc_ref[...] = jnp.zeros_like(acc_ref)

    for r in range(x_ref.shape[1] // FFN_ROWS):
        rows = slice(r * FFN_ROWS, (r + 1) * FFN_ROWS)
        x = x_ref[0, rows, :]
        a = _dot(x, wg)
        u = _dot(x, wu)
        hid = (a * jax.nn.sigmoid(a) * u).astype(BF16)
        acc_ref[rows, :] += _dot(hid, wd)

    @pl.when(f == pl.num_programs(1) - 1)
    def _():
        o_ref[0] = (acc_ref[...] * gate_ref[0]).astype(o_ref.dtype)


def expert_ffn(xin, gates, w_gate, w_up, w_down, l):
    tf = 256
    m = B * SLOTS
    return pl.pallas_call(
        _ffn_body,
        grid=(E, FE // tf),
        in_specs=[
            pl.BlockSpec((1, m, D), lambda e, f: (e, 0, 0)),
            pl.BlockSpec((1, 1, D, tf), lambda e, f: (l, e, 0, f)),
            pl.BlockSpec((1, 1, D, tf), lambda e, f: (l, e, 0, f)),
            pl.BlockSpec((1, 1, tf, D), lambda e, f: (l, e, f, 0)),
            pl.BlockSpec((1, m, 1), lambda e, f: (e, 0, 0)),
        ],
        out_specs=pl.BlockSpec((1, m, D), lambda e, f: (e, 0, 0)),
        out_shape=jax.ShapeDtypeStruct((E, m, D), BF16),
        scratch_shapes=[pltpu.VMEM((m, D), F32)],
        compiler_params=_cp("parallel", "arbitrary"),
        name="expert_ffn",
    )(xin.reshape(E, m, D), w_gate, w_up, w_down, gates.reshape(E, m, 1))


def _combine_body(o_ref, slot_ref, x_ref, g_ref, out_ref):
    b = pl.program_id(0)
    tt = pl.program_id(2)
    tn = out_ref.shape[1]

    def run(cap, o0, row):
        acc = jnp.zeros((TM, tn), F32)
        sl = lax.broadcasted_iota(I32, (TM, cap), 1)
        for e in range(E):
            onehot = jnp.where(slot_ref[0, :, e:e + 1] == sl, 1.0, 0.0).astype(BF16)
            acc = acc + _dot(onehot, o_ref[e, 0, o0:o0 + cap, :])
        out_ref[...] = x_ref[...] + g_ref[0, pl.ds(row, 1), :] * acc

    @pl.when(tt < LAT_TILES)
    def _():
        run(CAP_L, 0, b)

    @pl.when(tt == LAT_TILES)
    def _():
        run(CAP_C, CAP_L, CTX_ROW)


def combine(o, slot_tok, x2d, modt, l):
    tn = 1024
    per = D // tn
    return pl.pallas_call(
        _combine_body,
        grid=(B, per, TILES),
        in_specs=[
            pl.BlockSpec((E, 1, SLOTS, tn), lambda b, j, t: (0, b, 0, j)),
            pl.BlockSpec((1, TM, E), lambda b, j, t: (b, t, 0)),
            pl.BlockSpec((TM, tn), lambda b, j, t: (b * TILES + t, j)),
            pl.BlockSpec((1, 8, tn), lambda b, j, t: (l, 0, 5 * per + j)),
        ],
        out_specs=pl.BlockSpec((TM, tn), lambda b, j, t: (b * TILES + t, j)),
        out_shape=jax.ShapeDtypeStruct((R, D), F32),
        compiler_params=_cp("parallel", "parallel", "parallel"),
        name="moe_combine",
    )(o.reshape(E, B, SLOTS, D), slot_tok, x2d, modt)


def moe_block(x2d, gain, modt, l, w_router_t, w_gate, w_up, w_down):
    h2, slot, aff = router(x2d, gain, modt, l, w_router_t)
    xin, gates = dispatch(h2, slot, aff)
    o = expert_ffn(xin, gates, w_gate, w_up, w_down, l)
    return combine(o, jnp.swapaxes(slot, 1, 2), x2d, modt, l)


def _rope_tables(d):
    half = d // 2
    t = jnp.arange(S)
    inv = ROPE_BASE ** (-jnp.arange(0, half, 2, dtype=F32) / half)
    parts_c, parts_s = [], []
    for pos in (t // GRID_W, t % GRID_W):
        ang = pos.astype(F32)[:, None] * inv[None, :]
        parts_c += [jnp.cos(ang), jnp.cos(ang)]
        parts_s += [-jnp.sin(ang), jnp.sin(ang)]
    cos = jnp.concatenate([jnp.concatenate(parts_c, axis=1), jnp.ones((L, d), F32)], axis=0)
    sin = jnp.concatenate([jnp.concatenate(parts_s, axis=1), jnp.zeros((L, d), F32)], axis=0)
    return cos, sin


def _rope_lanes(x, cos, sin, quarter):
    lane = lax.broadcasted_iota(I32, x.shape, 1)
    sw = jnp.where(lane % (2 * quarter) < quarter, pltpu.roll(x, HD - quarter, 1), pltpu.roll(x, quarter, 1))
    return x * cos + sw * sin


def _table_spec():
    return pl.BlockSpec((TM, HD), lambda i: (i % TILES, 0))


def _post_ab_body(p_ref, c128_ref, s128_ref, c64_ref, s64_ref, qn_ref, kvn_ref,
                  aq_ref, ak_ref, av_ref, cq_ref, ckv_ref, kr_ref):
    cos, sin = c128_ref[...], s128_ref[...]
    for h in range(8):
        aq_ref[:, h * HD:(h + 1) * HD] = _rope_lanes(p_ref[:, h * HD:(h + 1) * HD], cos, sin, 32).astype(BF16)
    for h in range(2):
        ak_ref[:, h * HD:(h + 1) * HD] = _rope_lanes(p_ref[:, 1024 + h * HD:1024 + (h + 1) * HD], cos, sin,
                                                     32).astype(BF16)
    av_ref[...] = p_ref[:, 1280:1536].astype(BF16)
    cq_ref[...] = _rms(p_ref[:, 1536:2048], qn_ref[...]).astype(BF16)
    ckv_ref[...] = _rms(p_ref[:, 2048:2304], kvn_ref[...]).astype(BF16)
    kr = jnp.concatenate([p_ref[:, 2304:2368], jnp.zeros((TM, 64), F32)], axis=1)
    kr_ref[...] = _rope_lanes(kr, c64_ref[...], s64_ref[...], 16).astype(BF16)


def post_project_ab(proj, tabs128, tabs64pad, q_norm, kv_norm):
    widths = (1024, 256, 256, MLA_Q, MLA_KV, HD)
    return pl.pallas_call(
        _post_ab_body,
        grid=(R // TM,),
        in_specs=[pl.BlockSpec((TM, AB_IN), lambda i: (i, 0))] + [_table_spec()] * 4
        + [pl.BlockSpec((1, MLA_Q), lambda i: (0, 0)), pl.BlockSpec((1, MLA_KV), lambda i: (0, 0))],
        out_specs=[pl.BlockSpec((TM, w), lambda i: (i, 0)) for w in widths],
        out_shape=[jax.ShapeDtypeStruct((R, w), BF16) for w in widths],
        compiler_params=_cp("parallel"),
        name="post_project_ab",
    )(proj, *tabs128, *tabs64pad, q_norm.reshape(1, MLA_Q), kv_norm.reshape(1, MLA_KV))


def _q_up_body(cq_ref, w_ref, c64_ref, s64_ref, o_ref):
    y = _dot(cq_ref[...], w_ref[...])
    cos, sin = c64_ref[...], s64_ref[...]
    zeros = jnp.zeros((TM, 64), F32)
    for j in range(4):
        rot = _rope_lanes(y[:, 1024 + j * HD:1024 + (j + 1) * HD], cos, sin, 16)
        for u in range(2):
            h = 2 * j + u
            o_ref[:, 256 * h:256 * h + HD] = y[:, h * HD:(h + 1) * HD].astype(BF16)
            o_ref[:, 256 * h + HD:256 * (h + 1)] = jnp.concatenate([rot[:, 64 * u:64 * (u + 1)], zeros],
                                                                   axis=1).astype(BF16)


def mla_q_up(cq, w_uq_split, tabs64x2):
    return pl.pallas_call(
        _q_up_body,
        grid=(R // TM,),
        in_specs=[pl.BlockSpec((TM, MLA_Q), lambda i: (i, 0)), pl.BlockSpec((MLA_Q, 1536), lambda i: (0, 0)),
                  _table_spec(), _table_spec()],
        out_specs=pl.BlockSpec((TM, 2048), lambda i: (i, 0)),
        out_shape=jax.ShapeDtypeStruct((R, 2048), BF16),
        compiler_params=_cp("parallel"),
        name="mla_q_up",
    )(cq, w_uq_split, *tabs64x2)


def _kv_up_body(ckv_ref, w_ref, kr_ref, k_ref, v_ref):
    y = _dot(ckv_ref[...], w_ref[...])
    kr = kr_ref[...]
    for h in range(8):
        k_ref[:, 256 * h:256 * h + HD] = y[:, h * HD:(h + 1) * HD].astype(BF16)
        k_ref[:, 256 * h + HD:256 * (h + 1)] = kr
    v_ref[...] = y[:, 1024:].astype(BF16)


def mla_kv_up(ckv, w_ukv_split, kr):
    return pl.pallas_call(
        _kv_up_body,
        grid=(R // TM,),
        in_specs=[pl.BlockSpec((TM, MLA_KV), lambda i: (i, 0)), pl.BlockSpec((MLA_KV, 2048), lambda i: (0, 0)),
                  pl.BlockSpec((TM, HD), lambda i: (i, 0))],
        out_specs=[pl.BlockSpec((TM, 2048), lambda i: (i, 0)), pl.BlockSpec((TM, 1024), lambda i: (i, 0))],
        out_shape=[jax.ShapeDtypeStruct((R, 2048), BF16), jax.ShapeDtypeStruct((R, 1024), BF16)],
        compiler_params=_cp("parallel"),
        name="mla_kv_up",
    )(ckv, w_ukv_split, kr)


def _post_cd_body(p_ref, c128_ref, s128_ref, qn_ref, kn_ref, cq_ref, ck_ref, cv_ref, dq_ref, dk_ref, dv_ref):
    cos, sin = c128_ref[...], s128_ref[...]
    for h in range(8):
        x = _rms(p_ref[:, h * HD:(h + 1) * HD], qn_ref[...])
        cq_ref[:, h * HD:(h + 1) * HD] = _rope_lanes(x, cos, sin, 32).astype(BF16)
    for h in range(2):
        x = _rms(p_ref[:, 1024 + h * HD:1024 + (h + 1) * HD], kn_ref[...])
        ck_ref[:, h * HD:(h + 1) * HD] = _rope_lanes(x, cos, sin, 32).astype(BF16)
    cv_ref[...] = p_ref[:, 1280:1536].astype(BF16)
    dq_ref[...] = p_ref[:, 1536:2560].astype(BF16)
    dk_ref[...] = p_ref[:, 2560:3584].astype(BF16)
    dv_ref[...] = p_ref[:, 3584:4608].astype(BF16)


def post_project_cd(proj, tabs128, q_norm, k_norm):
    widths = (1024, 256, 256, 1024, 1024, 1024)
    return pl.pallas_call(
        _post_cd_body,
        grid=(R // TM,),
        in_specs=[pl.BlockSpec((TM, CD_IN), lambda i: (i, 0)), _table_spec(), _table_spec(),
                  pl.BlockSpec((1, HD), lambda i: (0, 0)), pl.BlockSpec((1, HD), lambda i: (0, 0))],
        out_specs=[pl.BlockSpec((TM, w), lambda i: (i, 0)) for w in widths],
        out_shape=[jax.ShapeDtypeStruct((R, w), BF16) for w in widths],
        compiler_params=_cp("parallel"),
        name="post_project_cd",
    )(proj, *tabs128, q_norm.reshape(1, HD), k_norm.reshape(1, HD))


def _sink_col(sink, kvh, g, tq):
    return jnp.broadcast_to(sink.astype(F32).reshape(kvh, g, 1, 1), (kvh, g, tq, 1)).reshape(kvh, g * tq, 1)


def _split_heads(w, n_first):
    k = w.shape[0]
    w3 = w.reshape(k, 8, -1)
    return jnp.concatenate([w3[:, :, :n_first].reshape(k, -1), w3[:, :, n_first:].reshape(k, -1)], axis=1)


def mixer_ab(h1, need_ctx, w_in, sink, q_norm, w_uq, kv_norm, w_ukv, tabs):
    proj = matmul(h1, w_in.astype(BF16), F32, 512, AB_IN)
    aq, ak, av, cq, ckv, kr = post_project_ab(proj, tabs["r128"], tabs["r64pad"], q_norm, kv_norm)
    bq = mla_q_up(cq, _split_heads(w_uq, 128).astype(BF16), tabs["r64x2"])
    bk, bv = mla_kv_up(ckv, _split_heads(w_ukv, 128).astype(BF16), kr)
    aq, ak, av, bq, bk, bv = (t.reshape(B, T, -1) for t in (aq, ak, av, bq, bk, bv))

    scale_a = HD ** -0.5
    scale_b = 192 ** -0.5
    ctx_blk = S // L
    att = jnp.zeros((B, T, D), BF16)
    att = window_attention(att, aq, ak, av, _sink_col(sink, 2, 4, BQ), kvh=2, g=4, scale=scale_a)
    att = dense_attention(att, bq, bk, bv, kvh=8, g=1, dk=256, dv=HD, tq=512, q_off=0, n_q=S // 512, nk=T,
                          k_blk=0, col_off=8, scale=scale_b)
    if need_ctx:
        att = dense_attention(att, aq, ak, av, kvh=2, g=4, dk=HD, dv=HD, tq=L, q_off=ctx_blk, n_q=1, nk=L,
                              k_blk=ctx_blk, col_off=0, scale=scale_a, sink_col=_sink_col(sink, 2, 4, L))
        att = dense_attention(att, bq, bk, bv, kvh=8, g=1, dk=256, dv=HD, tq=L, q_off=ctx_blk, n_q=1, nk=L,
                              k_blk=ctx_blk, col_off=8, scale=scale_b)
    return att.reshape(R, D)


def mixer_cd(h1, need_ctx, w_in, q_norm, k_norm, rpb, tabs):
    proj = matmul(h1, w_in.astype(BF16), F32, 512, CD_IN // 2)
    cq, ck, cv, dq, dk, dv = (t.reshape(B, T, -1)
                              for t in post_project_cd(proj, tabs["r128"], q_norm, k_norm))
    scale = HD ** -0.5
    ctx_blk = S // L
    att = jnp.zeros((B, T, D), BF16)
    att = dense_attention(att, cq, ck, cv, kvh=2, g=4, dk=HD, dv=HD, tq=BQ, q_off=0, n_q=S // BQ, nk=T, k_blk=0,
                          col_off=0, scale=scale)
    att = neighbourhood_attention(att, dq, dk, dv, nbr_bias_table(rpb), heads=8, col_off=8, scale=scale)
    if need_ctx:
        att = dense_attention(att, cq, ck, cv, kvh=2, g=4, dk=HD, dv=HD, tq=L, q_off=ctx_blk, n_q=1, nk=L,
                              k_blk=ctx_blk, col_off=0, scale=scale)
        att = dense_attention(att, dq, dk, dv, kvh=8, g=1, dk=HD, dv=HD, tq=L, q_off=ctx_blk, n_q=1, nk=L,
                              k_blk=ctx_blk, col_off=8, scale=scale)
    return att.reshape(R, D)


def _final_norm_body(x_ref, g_ref, o_ref):
    o_ref[0] = _rms(x_ref[0], g_ref[...])


def final_rmsnorm(x3, gain):
    return pl.pallas_call(
        _final_norm_body,
        grid=(B, S // TM),
        in_specs=[pl.BlockSpec((1, TM, D), lambda b, i: (b, i, 0)), pl.BlockSpec((1, D), lambda b, i: (0, 0))],
        out_specs=pl.BlockSpec((1, TM, D), lambda b, i: (b, i, 0)),
        out_shape=jax.ShapeDtypeStruct((B, S, D), F32),
        compiler_params=_cp("parallel", "parallel"),
        name="final_norm",
    )(x3, gain.reshape(1, D))


def kernel(x, c, ctx, c_ctx, w_ada, b_ada, norm1, norm2, ab_w_in, ab_sink, ab_q_norm, ab_w_uq, ab_kv_norm,
           ab_w_ukv, ab_w_out, cd_w_in, cd_q_norm, cd_k_norm, cd_rpb, cd_w_out, w_router, w_gate, w_up, w_down,
           final_norm):
    c8 = jnp.concatenate([c, c_ctx[None, :], jnp.zeros((8 - B - 1, D), F32)], axis=0)
    modt = ada_modulation(c8, w_ada, b_ada)
    c64, s64 = _rope_tables(64)
    zeros64 = jnp.zeros((T, 64), F32)
    tabs = {
        "r128": _rope_tables(HD),
        "r64x2": (jnp.concatenate([c64, c64], axis=1), jnp.concatenate([s64, s64], axis=1)),
        "r64pad": (jnp.concatenate([c64, zeros64], axis=1), jnp.concatenate([s64, zeros64], axis=1)),
    }
    x2d = jnp.concatenate([x, ctx], axis=1).reshape(R, D)
    for l in range(DEPTH):
        need_ctx = l < DEPTH - 1
        i = l // 2
        h1 = norm_modulate(x2d, norm1[l], modt, l, 1, 0)
        if l % 2 == 0:
            att = mixer_ab(h1, need_ctx, ab_w_in[i], ab_sink[i], ab_q_norm[i], ab_w_uq[i], ab_kv_norm[i],
                           ab_w_ukv[i], tabs)
            w_out = ab_w_out[i]
        else:
            att = mixer_cd(h1, need_ctx, cd_w_in[i], cd_q_norm[i], cd_k_norm[i], cd_rpb[i], tabs)
            w_out = cd_w_out[i]
        x2d = matmul_gated_residual(att, w_out.astype(BF16), x2d, modt, l, 2)
        x2d = moe_block(x2d, norm2[l], modt, l, w_router[l].T, w_gate, w_up, w_down)
    return final_rmsnorm(x2d.reshape(B, T, D), final_norm)
```

```python
import functools

import jax
import jax.numpy as jnp
from jax import lax
from jax.experimental import pallas as pl
from jax.experimental.pallas import tpu as pltpu

F32 = jnp.float32
BF16 = jnp.bfloat16
I32 = jnp.int32

D = 2048
B = 4
S = 2048
L = 256
T = S + L
R = B * T
DEPTH = 4
GRID_W = 64
HD = 128
ROPE_BASE = 10000.0
EPS = 1e-6
NEG = -1e30
LOG2E = 1.4426950408889634
QK_SCALE_LOG2 = HD ** -0.5 * LOG2E
MLA_SCALE_LOG2 = 192 ** -0.5 * LOG2E
WINDOW = 128
BQ = 128
NB_ROWS = 8
NB_COLS = 16
E = 16
CAP_L = 2 * S // E
CAP_C = 2 * L // E
SLOTS = CAP_L + CAP_C
FE = 1024
FFN_ROWS = 384
MLA_Q = 512
MLA_KV = 256
AB_IN = 2368
CD_IN = 4608

ATTN_SUB = 128
TM = 256
TILES = T // TM
LAT_TILES = S // TM
CTX_ROW = B
VMEM_LIMIT = 56 * 1024 * 1024


def _cp(*sem):
    return pltpu.CompilerParams(dimension_semantics=sem, vmem_limit_bytes=VMEM_LIMIT)


def _mod_row(tile):
    b = tile // TILES
    tt = tile % TILES
    return jnp.where(tt == LAT_TILES, CTX_ROW, b)


def _dot(a, b):
    return jnp.dot(a, b, preferred_element_type=F32)


def _dot_nt(a, b):
    return lax.dot_general(a, b, (((1,), (1,)), ((), ())), preferred_element_type=F32)


def _ada_body(c_ref, w_ref, b_ref, o_ref):
    c = c_ref[...]
    a = c * jax.nn.sigmoid(c)
    o_ref[0] = _dot(a.astype(BF16), w_ref[0].astype(BF16)) + b_ref[0]


def ada_modulation(c8, w_ada, b_ada):
    tn = 1024
    n = 6 * D
    return pl.pallas_call(
        _ada_body,
        grid=(DEPTH, n // tn),
        in_specs=[
            pl.BlockSpec((8, D), lambda l, j: (0, 0)),
            pl.BlockSpec((1, D, tn), lambda l, j: (l, 0, j)),
            pl.BlockSpec((1, 1, tn), lambda l, j: (l, 0, j)),
        ],
        out_specs=pl.BlockSpec((1, 8, tn), lambda l, j: (l, 0, j)),
        out_shape=jax.ShapeDtypeStruct((DEPTH, 8, n), F32),
        compiler_params=_cp("parallel", "parallel"),
        name="ada_modulation",
    )(c8, w_ada, b_ada.reshape(DEPTH, 1, n))


def _rms(x, gain):
    return x * lax.rsqrt(jnp.mean(x * x, axis=-1, keepdims=True) + EPS) * gain


def _normmod_body(x_ref, gain_ref, sc_ref, sh_ref, o_ref):
    row = _mod_row(pl.program_id(0))
    y = _rms(x_ref[...], gain_ref[...])
    o_ref[...] = (y * (1.0 + sc_ref[0, pl.ds(row, 1), :]) + sh_ref[0, pl.ds(row, 1), :]).astype(o_ref.dtype)


def norm_modulate(x2d, gain, modt, l, chunk_sc, chunk_sh):
    return pl.pallas_call(
        _normmod_body,
        grid=(R // TM,),
        in_specs=[
            pl.BlockSpec((TM, D), lambda i: (i, 0)),
            pl.BlockSpec((1, D), lambda i: (0, 0)),
            pl.BlockSpec((1, 8, D), lambda i: (l, 0, chunk_sc)),
            pl.BlockSpec((1, 8, D), lambda i: (l, 0, chunk_sh)),
        ],
        out_specs=pl.BlockSpec((TM, D), lambda i: (i, 0)),
        out_shape=jax.ShapeDtypeStruct((R, D), BF16),
        compiler_params=_cp("parallel"),
        name="norm_modulate",
    )(x2d, gain.reshape(1, D), modt, modt)


def _mm_body(a_ref, b_ref, o_ref):
    o_ref[...] = _dot(a_ref[...].astype(BF16), b_ref[...].astype(BF16)).astype(o_ref.dtype)


def matmul(a, b, out_dtype, tm, tn):
    m, k = a.shape
    n = b.shape[1]
    return pl.pallas_call(
        _mm_body,
        grid=(n // tn, m // tm),
        in_specs=[
            pl.BlockSpec((tm, k), lambda j, i: (i, 0)),
            pl.BlockSpec((k, tn), lambda j, i: (0, j)),
        ],
        out_specs=pl.BlockSpec((tm, tn), lambda j, i: (i, j)),
        out_shape=jax.ShapeDtypeStruct((m, n), out_dtype),
        compiler_params=_cp("parallel", "parallel"),
        name="matmul",
    )(a, b)


TM_BIG = T // 3


def _mm_res_body(a_ref, b_ref, x_ref, g_ref, o_ref):
    tile = pl.program_id(1)
    b = tile // 3
    last = tile % 3 == 2
    y = _dot(a_ref[...], b_ref[...])
    is_ctx = last & (lax.broadcasted_iota(I32, (TM_BIG, 1), 0) >= TM_BIG - L)
    gate = jnp.where(is_ctx, g_ref[0, CTX_ROW:CTX_ROW + 1, :], g_ref[0, pl.ds(b, 1), :])
    o_ref[...] = x_ref[...] + gate * y


def matmul_gated_residual(a, b, x2d, modt, l, chunk):
    tn = 1024
    k = a.shape[1]
    per = D // tn
    return pl.pallas_call(
        _mm_res_body,
        grid=(D // tn, R // TM_BIG),
        in_specs=[
            pl.BlockSpec((TM_BIG, k), lambda j, i: (i, 0)),
            pl.BlockSpec((k, tn), lambda j, i: (0, j)),
            pl.BlockSpec((TM_BIG, tn), lambda j, i: (i, j)),
            pl.BlockSpec((1, 8, tn), lambda j, i: (l, 0, chunk * per + j)),
        ],
        out_specs=pl.BlockSpec((TM_BIG, tn), lambda j, i: (i, j)),
        out_shape=jax.ShapeDtypeStruct((R, D), F32),
        compiler_params=_cp("parallel", "parallel"),
        name="matmul_gated_residual",
    )(a, b, x2d, modt)


def _softmax_pv(parts, sink, rows):
    m = parts[0][0].max(axis=-1, keepdims=True)
    for s, _ in parts[1:]:
        m = jnp.maximum(m, s.max(axis=-1, keepdims=True))
    if sink is not None:
        m = jnp.maximum(m, sink)
    den = jnp.zeros((rows, 1), F32) if sink is None else jnp.exp2(sink - m)
    acc = None
    for s, v in parts:
        p = jnp.exp2(s - m)
        den = den + p.sum(axis=-1, keepdims=True)
        pv = _dot(p.astype(BF16), v)
        acc = pv if acc is None else acc + pv
    return acc / den


def _dense_attn_body(*refs, g, tq, dk, dv, has_sink):
    if has_sink:
        q_ref, k_ref, v_ref, sink_ref, _, o_ref = refs
        sink = sink_ref[0]
    else:
        q_ref, k_ref, v_ref, _, o_ref = refs
        sink = None
    sub = min(tq, ATTN_SUB) if g > 1 else tq
    for j in range(g):
        for r in range(tq // sub):
            rows = slice(r * sub, (r + 1) * sub)
            s = _dot_nt(q_ref[0, rows, j * dk:(j + 1) * dk], k_ref[0])
            sk = None if sink is None else sink[j * tq + r * sub:j * tq + (r + 1) * sub]
            o = _softmax_pv([(s, v_ref[0])], sk, sub)
            o_ref[0, rows, j * dv:(j + 1) * dv] = o.astype(o_ref.dtype)


def dense_attention(att, q, k, v, *, kvh, g, dk, dv, tq, q_off, n_q, nk, k_blk, col_off, sink_col=None):
    in_specs = [
        pl.BlockSpec((1, tq, g * dk), lambda b, h, i: (b, i + q_off, h)),
        pl.BlockSpec((1, nk, dk), lambda b, h, i: (b, k_blk, h)),
        pl.BlockSpec((1, nk, dv), lambda b, h, i: (b, k_blk, h)),
    ]
    args = [q, k, v]
    if sink_col is not None:
        in_specs.append(pl.BlockSpec((1, g * tq, 1), lambda b, h, i: (h, 0, 0)))
        args.append(sink_col)
    in_specs.append(pl.BlockSpec(memory_space=pl.ANY))
    args.append(att)
    return pl.pallas_call(
        functools.partial(_dense_attn_body, g=g, tq=tq, dk=dk, dv=dv, has_sink=sink_col is not None),
        grid=(B, kvh, n_q),
        in_specs=in_specs,
        out_specs=pl.BlockSpec((1, tq, g * dv), lambda b, h, i: (b, i + q_off, h + col_off)),
        out_shape=jax.ShapeDtypeStruct(att.shape, att.dtype),
        input_output_aliases={len(args) - 1: 0},
        compiler_params=_cp("parallel", "parallel", "parallel"),
        name="dense_attention",
    )(*args)


def _window_attn_body(q_ref, kp_ref, kc_ref, kn_ref, kx_ref, vp_ref, vc_ref, vn_ref, vx_ref, sink_ref,
                      _, o_ref, *, g):
    n = pl.program_id(2)
    nb = S // BQ
    kl = jnp.concatenate([kp_ref[0], kc_ref[0], kn_ref[0]], axis=0)
    vl = jnp.concatenate([vp_ref[0], vc_ref[0], vn_ref[0]], axis=0)
    qi = lax.broadcasted_iota(I32, (BQ, 3 * BQ), 0) + BQ
    kj = lax.broadcasted_iota(I32, (BQ, 3 * BQ), 1)
    lo = jnp.where(n > 0, 0, BQ)
    hi = jnp.where(n < nb - 1, 3 * BQ, 2 * BQ)
    valid = (jnp.abs(qi - kj) <= WINDOW) & (kj >= lo) & (kj < hi)
    for j in range(g):
        q = q_ref[0, :, j * HD:(j + 1) * HD]
        s_loc = jnp.where(valid, _dot_nt(q, kl), NEG)
        s_ctx = _dot_nt(q, kx_ref[0])
        o = _softmax_pv([(s_loc, vl), (s_ctx, vx_ref[0])], sink_ref[0, j * BQ:(j + 1) * BQ], BQ)
        o_ref[0, :, j * HD:(j + 1) * HD] = o.astype(o_ref.dtype)


def window_attention(att, q, k, v, sink_col, *, kvh, g):
    nb = S // BQ
    ctx_blk = S // L

    def kv_specs():
        return [
            pl.BlockSpec((1, BQ, HD), lambda b, h, n: (b, jnp.maximum(n - 1, 0), h)),
            pl.BlockSpec((1, BQ, HD), lambda b, h, n: (b, n, h)),
            pl.BlockSpec((1, BQ, HD), lambda b, h, n: (b, jnp.minimum(n + 1, nb - 1), h)),
            pl.BlockSpec((1, L, HD), lambda b, h, n: (b, ctx_blk, h)),
        ]

    return pl.pallas_call(
        functools.partial(_window_attn_body, g=g),
        grid=(B, kvh, nb),
        in_specs=[pl.BlockSpec((1, BQ, g * HD), lambda b, h, n: (b, n, h))] + kv_specs() + kv_specs()
        + [pl.BlockSpec((1, g * BQ, 1), lambda b, h, n: (h, 0, 0)), pl.BlockSpec(memory_space=pl.ANY)],
        out_specs=pl.BlockSpec((1, BQ, g * HD), lambda b, h, n: (b, n, h)),
        out_shape=jax.ShapeDtypeStruct(att.shape, att.dtype),
        input_output_aliases={10: 0},
        compiler_params=_cp("parallel", "parallel", "parallel"),
        name="window_attention",
    )(q, k, k, k, k, v, v, v, v, sink_col, att)


ROWS_N = S // GRID_W
NBR_G = 4
NBR_KR = NBR_G + NB_ROWS
NBR_GROUPS = ROWS_N // NBR_G
NBR_PATTERN_GROUPS = (0, 1, NBR_GROUPS - 1)


def _nbr_window_start(grp):
    return NBR_G * grp - NB_ROWS // 2


def _nbr_bias_body(rpb_ref, o_ref):
    h = pl.program_id(0)
    n_dc = 2 * NB_COLS - 1
    n_dr = 2 * NB_ROWS - 1
    qc = lax.broadcasted_iota(I32, (GRID_W, GRID_W), 0)
    kc = lax.broadcasted_iota(I32, (GRID_W, GRID_W), 1)
    dc = jnp.clip(kc - qc + NB_COLS - 1, 0, n_dc - 1)
    cs = jnp.clip(qc - NB_COLS // 2, 0, GRID_W - NB_COLS)
    col_ok = (kc >= cs) & (kc < cs + NB_COLS)
    tiles = [jnp.zeros((GRID_W, GRID_W), F32) for _ in range(n_dr)]
    for t in range(n_dc):
        hit = dc == t
        for dr in range(n_dr):
            tiles[dr] = jnp.where(hit, rpb_ref[h, dr * n_dc + t] * LOG2E, tiles[dr])
    tiles = [jnp.where(col_ok, x, NEG) for x in tiles]
    masked = jnp.full((GRID_W, GRID_W), NEG, F32)
    for p, grp in enumerate(NBR_PATTERN_GROUPS):
        start = min(max(_nbr_window_start(grp), 0), ROWS_N - NBR_KR)
        for ri in range(NBR_G):
            r = NBR_G * grp + ri
            rs = min(max(r - NB_ROWS // 2, 0), ROWS_N - NB_ROWS)
            row = [tiles[start + kr - r + NB_ROWS - 1] if rs <= start + kr < rs + NB_ROWS else masked
                   for kr in range(NBR_KR)]
            o_ref[0, p, ri * GRID_W:(ri + 1) * GRID_W, :] = jnp.concatenate(row, axis=1)


def nbr_bias_table(rpb):
    heads = rpb.shape[0]
    return pl.pallas_call(
        _nbr_bias_body,
        grid=(heads,),
        in_specs=[pl.BlockSpec(memory_space=pltpu.SMEM)],
        out_specs=pl.BlockSpec((1, 3, NBR_G * GRID_W, NBR_KR * GRID_W), lambda h: (h, 0, 0, 0)),
        out_shape=jax.ShapeDtypeStruct((heads, 3, NBR_G * GRID_W, NBR_KR * GRID_W), F32),
        compiler_params=_cp("parallel"),
        name="nbr_bias_table",
    )(rpb.astype(F32).reshape(heads, -1))


def _nbr_attn_body(q_ref, k_ref, v_ref, bias_ref, _, o_ref):
    grp = pl.program_id(2)
    n_loc = NBR_KR * GRID_W
    start_row = jnp.clip(_nbr_window_start(grp), 0, ROWS_N - NBR_KR)
    start = pl.multiple_of(start_row * GRID_W, GRID_W)
    q = q_ref[0]
    s_loc = _dot_nt(q, k_ref[0, pl.ds(start, n_loc), :]) + bias_ref[0, 0]
    s_ctx = _dot_nt(q, k_ref[0, S:T, :])
    o = _softmax_pv([(s_loc, v_ref[0, pl.ds(start, n_loc), :]), (s_ctx, v_ref[0, S:T, :])], None,
                    NBR_G * GRID_W)
    o_ref[0] = o.astype(o_ref.dtype)


def neighbourhood_attention(att, q, k, v, bias_tab, *, heads, col_off):
    tq = NBR_G * GRID_W

    def pattern(grp):
        return jnp.where(grp == 0, 0, jnp.where(grp == NBR_GROUPS - 1, 2, 1))

    return pl.pallas_call(
        _nbr_attn_body,
        grid=(B, heads, NBR_GROUPS),
        in_specs=[
            pl.BlockSpec((1, tq, HD), lambda b, h, r: (b, r, h)),
            pl.BlockSpec((1, T, HD), lambda b, h, r: (b, 0, h)),
            pl.BlockSpec((1, T, HD), lambda b, h, r: (b, 0, h)),
            pl.BlockSpec((1, 1, tq, NBR_KR * GRID_W), lambda b, h, r: (h, pattern(r), 0, 0)),
            pl.BlockSpec(memory_space=pl.ANY),
        ],
        out_specs=pl.BlockSpec((1, tq, HD), lambda b, h, r: (b, r, h + col_off)),
        out_shape=jax.ShapeDtypeStruct(att.shape, att.dtype),
        input_output_aliases={4: 0},
        compiler_params=_cp("parallel", "parallel", "parallel"),
        name="neighbourhood_attention",
    )(q, k, v, bias_tab, att)


def _cumsum_lanes(mask, tri):
    n = mask.shape[1]
    carry = jnp.zeros((mask.shape[0], 1), F32)
    out = []
    for j in range(n // 256):
        blk = jnp.where(mask[:, j * 256:(j + 1) * 256], 1.0, 0.0).astype(BF16)
        c = _dot(blk, tri) + carry
        carry = c[:, 255:256]
        out.append(c)
    return jnp.concatenate(out, axis=1) if len(out) > 1 else out[0]


def _count(mask):
    return jnp.sum(jnp.where(mask, 1.0, 0.0), axis=1, keepdims=True)


def _topk_slots(affs, caps, tri):
    rows = affs[0].shape[0]

    def step(carry):
        bounds, _ = carry
        new, moved = [], []
        for aff, cap, (lo, hi) in zip(affs, caps, bounds):
            mid = 0.5 * (lo + hi)
            ok = _count(aff >= mid) >= cap
            nlo = jnp.where(ok, mid, lo)
            nhi = jnp.where(ok, hi, mid)
            new.append((nlo, nhi))
            moved.append(jnp.where((nlo != lo) | (nhi != hi), 1.0, 0.0))
        return tuple(new), jnp.max(functools.reduce(jnp.maximum, moved))

    start = tuple((jnp.zeros((rows, 1), F32), jnp.full((rows, 1), 2.0, F32)) for _ in affs)
    bounds, _ = lax.while_loop(lambda c: c[1] > 0.0, step, (start, jnp.float32(1.0)))
    slots = []
    for aff, cap, (lo, hi) in zip(affs, caps, bounds):
        gt = aff >= hi
        eq = (aff >= lo) & (aff < hi)
        need = cap - _count(gt)
        sel = gt | (eq & (_cumsum_lanes(eq, tri) <= need))
        pos = _cumsum_lanes(sel, tri) - 1.0
        slots.append(jnp.where(sel, pos.astype(I32), -1))
    return slots


def _router_body(x_ref, gain_ref, sc_ref, sh_ref, wr_ref, h_ref, slot_ref, aff_ref, lg_scr):
    b = pl.program_id(0)
    tt = pl.program_id(1)
    row = jnp.where(tt == LAT_TILES, CTX_ROW, b)
    y = _rms(x_ref[...], gain_ref[...])
    h = y * (1.0 + sc_ref[0, pl.ds(row, 1), :]) + sh_ref[0, pl.ds(row, 1), :]
    h_ref[...] = h.astype(BF16)
    w = wr_ref[...]
    w_hi, h_hi = w.astype(BF16), h.astype(BF16)
    w_lo = (w - w_hi.astype(F32)).astype(BF16)
    h_lo = (h - h_hi.astype(F32)).astype(BF16)
    lg_scr[tt] = _dot_nt(w_hi, h_hi) + (_dot_nt(w_hi, h_lo) + _dot_nt(w_lo, h_hi))

    @pl.when(tt == TILES - 1)
    def _():
        tri = jnp.where(lax.broadcasted_iota(I32, (256, 256), 0) <= lax.broadcasted_iota(I32, (256, 256), 1),
                        1.0, 0.0).astype(BF16)
        lat = jnp.concatenate([lg_scr[j] for j in range(LAT_TILES)], axis=1)
        affs = []
        for lg in (lat, lg_scr[LAT_TILES]):
            ex = jnp.exp(lg - lg.max(axis=0, keepdims=True))
            affs.append(ex / ex.sum(axis=0, keepdims=True))
        slots = _topk_slots(affs, (CAP_L, CAP_C), tri)
        for (off, n, _, _), aff, slot in zip(_SETS, affs, slots):
            slot_ref[0, :, off:off + n] = slot
            aff_ref[0, :, off:off + n] = aff


def router(x2d, gain, modt, l, w_router_t):
    return pl.pallas_call(
        _router_body,
        grid=(B, TILES),
        in_specs=[
            pl.BlockSpec((TM, D), lambda b, t: (b * TILES + t, 0)),
            pl.BlockSpec((1, D), lambda b, t: (0, 0)),
            pl.BlockSpec((1, 8, D), lambda b, t: (l, 0, 4)),
            pl.BlockSpec((1, 8, D), lambda b, t: (l, 0, 3)),
            pl.BlockSpec((E, D), lambda b, t: (0, 0)),
        ],
        out_specs=[
            pl.BlockSpec((TM, D), lambda b, t: (b * TILES + t, 0)),
            pl.BlockSpec((1, E, T), lambda b, t: (b, 0, 0)),
            pl.BlockSpec((1, E, T), lambda b, t: (b, 0, 0)),
        ],
        out_shape=[
            jax.ShapeDtypeStruct((R, D), BF16),
            jax.ShapeDtypeStruct((B, E, T), I32),
            jax.ShapeDtypeStruct((B, E, T), F32),
        ],
        scratch_shapes=[pltpu.VMEM((TILES, E, TM), F32)],
        compiler_params=_cp("parallel", "arbitrary"),
        name="router",
    )(x2d, gain.reshape(1, D), modt, modt, w_router_t)


_SETS = ((0, S, CAP_L, 0), (S, L, CAP_C, CAP_L))


def _dispatch_body(h_ref, slot_ref, aff_ref, xin_ref, gate_ref):
    e = pl.program_id(1)
    srow = slot_ref[0, pl.ds(e, 1), :]
    arow = aff_ref[0, pl.ds(e, 1), :]
    for off, n, cap, o0 in _SETS:
        hit = lax.broadcasted_iota(I32, (cap, n), 0) == srow[:, off:off + n]
        onehot = jnp.where(hit, 1.0, 0.0).astype(BF16)
        xin_ref[0, 0, o0:o0 + cap, :] = _dot(onehot, h_ref[0, off:off + n, :]).astype(BF16)
        gate_ref[0, 0, o0:o0 + cap, :] = jnp.sum(jnp.where(hit, arow[:, off:off + n], 0.0), axis=1, keepdims=True)


def dispatch(h2, slot, aff):
    return pl.pallas_call(
        _dispatch_body,
        grid=(B, E),
        in_specs=[
            pl.BlockSpec((1, T, D), lambda b, e: (b, 0, 0)),
            pl.BlockSpec((1, E, T), lambda b, e: (b, 0, 0)),
            pl.BlockSpec((1, E, T), lambda b, e: (b, 0, 0)),
        ],
        out_specs=[
            pl.BlockSpec((1, 1, SLOTS, D), lambda b, e: (e, b, 0, 0)),
            pl.BlockSpec((1, 1, SLOTS, 1), lambda b, e: (e, b, 0, 0)),
        ],
        out_shape=[
            jax.ShapeDtypeStruct((E, B, SLOTS, D), BF16),
            jax.ShapeDtypeStruct((E, B, SLOTS, 1), F32),
        ],
        compiler_params=_cp("parallel", "parallel"),
        name="moe_dispatch",
    )(h2.reshape(B, T, D), slot, aff)


def _ffn_body(x_ref, wg_ref, wu_ref, wd_ref, gate_ref, o_ref, acc_ref):
    f = pl.program_id(1)
    wg = wg_ref[0, 0].astype(BF16)
    wu = wu_ref[0, 0].astype(BF16)
    wd = wd_ref[0, 0].astype(BF16)

    @pl.when(f == 0)
    def _():
        acc_ref[...] = jnp.zeros_like(acc_ref)

    for r in range(x_ref.shape[1] // FFN_ROWS):
        rows = slice(r * FFN_ROWS, (r + 1) * FFN_ROWS)
        x = x_ref[0, rows, :]
        a = _dot(x, wg)
        u = _dot(x, wu)
        hid = (a * jax.nn.sigmoid(a) * u).astype(BF16)
        acc_ref[rows, :] += _dot(hid, wd)

    @pl.when(f == pl.num_programs(1) - 1)
    def _():
        o_ref[0] = (acc_ref[...] * gate_ref[0]).astype(o_ref.dtype)


def expert_ffn(xin, gates, w_gate, w_up, w_down, l):
    tf = 256
    m = B * SLOTS
    return pl.pallas_call(
        _ffn_body,
        grid=(E, FE // tf),
        in_specs=[
            pl.BlockSpec((1, m, D), lambda e, f: (e, 0, 0)),
            pl.BlockSpec((1, 1, D, tf), lambda e, f: (l, e, 0, f)),
            pl.BlockSpec((1, 1, D, tf), lambda e, f: (l, e, 0, f)),
            pl.BlockSpec((1, 1, tf, D), lambda e, f: (l, e, f, 0)),
            pl.BlockSpec((1, m, 1), lambda e, f: (e, 0, 0)),
        ],
        out_specs=pl.BlockSpec((1, m, D), lambda e, f: (e, 0, 0)),
        out_shape=jax.ShapeDtypeStruct((E, m, D), BF16),
        scratch_shapes=[pltpu.VMEM((m, D), F32)],
        compiler_params=_cp("parallel", "arbitrary"),
        name="expert_ffn",
    )(xin.reshape(E, m, D), w_gate, w_up, w_down, gates.reshape(E, m, 1))


def _combine_body(o_ref, slot_ref, x_ref, g_ref, out_ref):
    b = pl.program_id(0)
    tt = pl.program_id(2)
    tn = out_ref.shape[1]

    def run(cap, o0, row):
        acc = jnp.zeros((TM, tn), F32)
        sl = lax.broadcasted_iota(I32, (TM, cap), 1)
        for e in range(E):
            onehot = jnp.where(slot_ref[0, :, e:e + 1] == sl, 1.0, 0.0).astype(BF16)
            acc = acc + _dot(onehot, o_ref[e, 0, o0:o0 + cap, :])
        out_ref[...] = x_ref[...] + g_ref[0, pl.ds(row, 1), :] * acc

    @pl.when(tt < LAT_TILES)
    def _():
        run(CAP_L, 0, b)

    @pl.when(tt == LAT_TILES)
    def _():
        run(CAP_C, CAP_L, CTX_ROW)


def combine(o, slot_tok, x2d, modt, l):
    tn = 1024
    per = D // tn
    return pl.pallas_call(
        _combine_body,
        grid=(B, per, TILES),
        in_specs=[
            pl.BlockSpec((E, 1, SLOTS, tn), lambda b, j, t: (0, b, 0, j)),
            pl.BlockSpec((1, TM, E), lambda b, j, t: (b, t, 0)),
            pl.BlockSpec((TM, tn), lambda b, j, t: (b * TILES + t, j)),
            pl.BlockSpec((1, 8, tn), lambda b, j, t: (l, 0, 5 * per + j)),
        ],
        out_specs=pl.BlockSpec((TM, tn), lambda b, j, t: (b * TILES + t, j)),
        out_shape=jax.ShapeDtypeStruct((R, D), F32),
        compiler_params=_cp("parallel", "parallel", "parallel"),
        name="moe_combine",
    )(o.reshape(E, B, SLOTS, D), slot_tok, x2d, modt)


def moe_block(x2d, gain, modt, l, w_router_t, w_gate, w_up, w_down):
    h2, slot, aff = router(x2d, gain, modt, l, w_router_t)
    xin, gates = dispatch(h2, slot, aff)
    o = expert_ffn(xin, gates, w_gate, w_up, w_down, l)
    return combine(o, jnp.swapaxes(slot, 1, 2), x2d, modt, l)


def _rope_tables(d):
    half = d // 2
    t = jnp.arange(S)
    inv = ROPE_BASE ** (-jnp.arange(0, half, 2, dtype=F32) / half)
    parts_c, parts_s = [], []
    for pos in (t // GRID_W, t % GRID_W):
        ang = pos.astype(F32)[:, None] * inv[None, :]
        parts_c += [jnp.cos(ang), jnp.cos(ang)]
        parts_s += [-jnp.sin(ang), jnp.sin(ang)]
    cos = jnp.concatenate([jnp.concatenate(parts_c, axis=1), jnp.ones((L, d), F32)], axis=0)
    sin = jnp.concatenate([jnp.concatenate(parts_s, axis=1), jnp.zeros((L, d), F32)], axis=0)
    return cos, sin


def _rope_lanes(x, cos, sin, quarter):
    lane = lax.broadcasted_iota(I32, x.shape, 1)
    sw = jnp.where(lane % (2 * quarter) < quarter, pltpu.roll(x, HD - quarter, 1), pltpu.roll(x, quarter, 1))
    return x * cos + sw * sin


def _table_spec():
    return pl.BlockSpec((TM, HD), lambda i: (i % TILES, 0))


def _post_ab_body(p_ref, c128_ref, s128_ref, c64_ref, s64_ref, qn_ref, kvn_ref,
                  aq_ref, ak_ref, av_ref, cq_ref, ckv_ref, kr_ref):
    cos, sin = c128_ref[...], s128_ref[...]
    for h in range(8):
        aq = _rope_lanes(p_ref[:, h * HD:(h + 1) * HD], cos, sin, 32) * QK_SCALE_LOG2
        aq_ref[:, h * HD:(h + 1) * HD] = aq.astype(BF16)
    for h in range(2):
        ak_ref[:, h * HD:(h + 1) * HD] = _rope_lanes(p_ref[:, 1024 + h * HD:1024 + (h + 1) * HD], cos, sin,
                                                     32).astype(BF16)
    av_ref[...] = p_ref[:, 1280:1536].astype(BF16)
    cq_ref[...] = _rms(p_ref[:, 1536:2048], qn_ref[...]).astype(BF16)
    ckv_ref[...] = _rms(p_ref[:, 2048:2304], kvn_ref[...]).astype(BF16)
    kr = jnp.concatenate([p_ref[:, 2304:2368], jnp.zeros((TM, 64), F32)], axis=1)
    kr_ref[...] = _rope_lanes(kr, c64_ref[...], s64_ref[...], 16).astype(BF16)


def post_project_ab(proj, tabs128, tabs64pad, q_norm, kv_norm):
    widths = (1024, 256, 256, MLA_Q, MLA_KV, HD)
    return pl.pallas_call(
        _post_ab_body,
        grid=(R // TM,),
        in_specs=[pl.BlockSpec((TM, AB_IN), lambda i: (i, 0))] + [_table_spec()] * 4
        + [pl.BlockSpec((1, MLA_Q), lambda i: (0, 0)), pl.BlockSpec((1, MLA_KV), lambda i: (0, 0))],
        out_specs=[pl.BlockSpec((TM, w), lambda i: (i, 0)) for w in widths],
        out_shape=[jax.ShapeDtypeStruct((R, w), BF16) for w in widths],
        compiler_params=_cp("parallel"),
        name="post_project_ab",
    )(proj, *tabs128, *tabs64pad, q_norm.reshape(1, MLA_Q), kv_norm.reshape(1, MLA_KV))


def _q_up_body(cq_ref, w_ref, c64_ref, s64_ref, o_ref):
    y = _dot(cq_ref[...], w_ref[...])
    cos, sin = c64_ref[...], s64_ref[...]
    zeros = jnp.zeros((TM, 64), F32)
    for j in range(4):
        rot = _rope_lanes(y[:, 1024 + j * HD:1024 + (j + 1) * HD], cos, sin, 16)
        for u in range(2):
            h = 2 * j + u
            o_ref[:, 256 * h:256 * h + HD] = (y[:, h * HD:(h + 1) * HD] * MLA_SCALE_LOG2).astype(BF16)
            o_ref[:, 256 * h + HD:256 * (h + 1)] = jnp.concatenate(
                [rot[:, 64 * u:64 * (u + 1)] * MLA_SCALE_LOG2, zeros], axis=1).astype(BF16)


def mla_q_up(cq, w_uq_split, tabs64x2):
    return pl.pallas_call(
        _q_up_body,
        grid=(R // TM,),
        in_specs=[pl.BlockSpec((TM, MLA_Q), lambda i: (i, 0)), pl.BlockSpec((MLA_Q, 1536), lambda i: (0, 0)),
                  _table_spec(), _table_spec()],
        out_specs=pl.BlockSpec((TM, 2048), lambda i: (i, 0)),
        out_shape=jax.ShapeDtypeStruct((R, 2048), BF16),
        compiler_params=_cp("parallel"),
        name="mla_q_up",
    )(cq, w_uq_split, *tabs64x2)


def _kv_up_body(ckv_ref, w_ref, kr_ref, k_ref, v_ref):
    y = _dot(ckv_ref[...], w_ref[...])
    kr = kr_ref[...]
    for h in range(8):
        k_ref[:, 256 * h:256 * h + HD] = y[:, h * HD:(h + 1) * HD].astype(BF16)
        k_ref[:, 256 * h + HD:256 * (h + 1)] = kr
    v_ref[...] = y[:, 1024:].astype(BF16)


def mla_kv_up(ckv, w_ukv_split, kr):
    return pl.pallas_call(
        _kv_up_body,
        grid=(R // TM,),
        in_specs=[pl.BlockSpec((TM, MLA_KV), lambda i: (i, 0)), pl.BlockSpec((MLA_KV, 2048), lambda i: (0, 0)),
                  pl.BlockSpec((TM, HD), lambda i: (i, 0))],
        out_specs=[pl.BlockSpec((TM, 2048), lambda i: (i, 0)), pl.BlockSpec((TM, 1024), lambda i: (i, 0))],
        out_shape=[jax.ShapeDtypeStruct((R, 2048), BF16), jax.ShapeDtypeStruct((R, 1024), BF16)],
        compiler_params=_cp("parallel"),
        name="mla_kv_up",
    )(ckv, w_ukv_split, kr)


def _post_cd_body(p_ref, c128_ref, s128_ref, qn_ref, kn_ref, cq_ref, ck_ref, cv_ref, dq_ref, dk_ref, dv_ref):
    cos, sin = c128_ref[...], s128_ref[...]
    for h in range(8):
        x = _rms(p_ref[:, h * HD:(h + 1) * HD], qn_ref[...])
        cq_ref[:, h * HD:(h + 1) * HD] = (_rope_lanes(x, cos, sin, 32) * QK_SCALE_LOG2).astype(BF16)
    for h in range(2):
        x = _rms(p_ref[:, 1024 + h * HD:1024 + (h + 1) * HD], kn_ref[...])
        ck_ref[:, h * HD:(h + 1) * HD] = _rope_lanes(x, cos, sin, 32).astype(BF16)
    cv_ref[...] = p_ref[:, 1280:1536].astype(BF16)
    dq_ref[...] = (p_ref[:, 1536:2560] * QK_SCALE_LOG2).astype(BF16)
    dk_ref[...] = p_ref[:, 2560:3584].astype(BF16)
    dv_ref[...] = p_ref[:, 3584:4608].astype(BF16)


def post_project_cd(proj, tabs128, q_norm, k_norm):
    widths = (1024, 256, 256, 1024, 1024, 1024)
    return pl.pallas_call(
        _post_cd_body,
        grid=(R // TM,),
        in_specs=[pl.BlockSpec((TM, CD_IN), lambda i: (i, 0)), _table_spec(), _table_spec(),
                  pl.BlockSpec((1, HD), lambda i: (0, 0)), pl.BlockSpec((1, HD), lambda i: (0, 0))],
        out_specs=[pl.BlockSpec((TM, w), lambda i: (i, 0)) for w in widths],
        out_shape=[jax.ShapeDtypeStruct((R, w), BF16) for w in widths],
        compiler_params=_cp("parallel"),
        name="post_project_cd",
    )(proj, *tabs128, q_norm.reshape(1, HD), k_norm.reshape(1, HD))


def _sink_col(sink, kvh, g, tq):
    col = (sink.astype(F32) * LOG2E).reshape(kvh, g, 1, 1)
    return jnp.broadcast_to(col, (kvh, g, tq, 1)).reshape(kvh, g * tq, 1)


def _split_heads(w, n_first):
    k = w.shape[0]
    w3 = w.reshape(k, 8, -1)
    return jnp.concatenate([w3[:, :, :n_first].reshape(k, -1), w3[:, :, n_first:].reshape(k, -1)], axis=1)


def mixer_ab(h1, need_ctx, w_in, sink, q_norm, w_uq, kv_norm, w_ukv, tabs):
    proj = matmul(h1, w_in.astype(BF16), F32, 512, AB_IN)
    aq, ak, av, cq, ckv, kr = post_project_ab(proj, tabs["r128"], tabs["r64pad"], q_norm, kv_norm)
    bq = mla_q_up(cq, _split_heads(w_uq, 128).astype(BF16), tabs["r64x2"])
    bk, bv = mla_kv_up(ckv, _split_heads(w_ukv, 128).astype(BF16), kr)
    aq, ak, av, bq, bk, bv = (t.reshape(B, T, -1) for t in (aq, ak, av, bq, bk, bv))

    ctx_blk = S // L
    att = jnp.zeros((B, T, D), BF16)
    att = window_attention(att, aq, ak, av, _sink_col(sink, 2, 4, BQ), kvh=2, g=4)
    att = dense_attention(att, bq, bk, bv, kvh=8, g=1, dk=256, dv=HD, tq=512, q_off=0, n_q=S // 512, nk=T,
                          k_blk=0, col_off=8)
    if need_ctx:
        att = dense_attention(att, aq, ak, av, kvh=2, g=4, dk=HD, dv=HD, tq=L, q_off=ctx_blk, n_q=1, nk=L,
                              k_blk=ctx_blk, col_off=0, sink_col=_sink_col(sink, 2, 4, L))
        att = dense_attention(att, bq, bk, bv, kvh=8, g=1, dk=256, dv=HD, tq=L, q_off=ctx_blk, n_q=1, nk=L,
                              k_blk=ctx_blk, col_off=8)
    return att.reshape(R, D)


def mixer_cd(h1, need_ctx, w_in, q_norm, k_norm, rpb, tabs):
    proj = matmul(h1, w_in.astype(BF16), F32, 512, CD_IN // 2)
    cq, ck, cv, dq, dk, dv = (t.reshape(B, T, -1)
                              for t in post_project_cd(proj, tabs["r128"], q_norm, k_norm))
    ctx_blk = S // L
    att = jnp.zeros((B, T, D), BF16)
    att = dense_attention(att, cq, ck, cv, kvh=2, g=4, dk=HD, dv=HD, tq=BQ, q_off=0, n_q=S // BQ, nk=T, k_blk=0,
                          col_off=0)
    att = neighbourhood_attention(att, dq, dk, dv, nbr_bias_table(rpb), heads=8, col_off=8)
    if need_ctx:
        att = dense_attention(att, cq, ck, cv, kvh=2, g=4, dk=HD, dv=HD, tq=L, q_off=ctx_blk, n_q=1, nk=L,
                              k_blk=ctx_blk, col_off=0)
        att = dense_attention(att, dq, dk, dv, kvh=8, g=1, dk=HD, dv=HD, tq=L, q_off=ctx_blk, n_q=1, nk=L,
                              k_blk=ctx_blk, col_off=8)
    return att.reshape(R, D)


def _final_norm_body(x_ref, g_ref, o_ref):
    o_ref[0] = _rms(x_ref[0], g_ref[...])


def final_rmsnorm(x3, gain):
    return pl.pallas_call(
        _final_norm_body,
        grid=(B, S // TM),
        in_specs=[pl.BlockSpec((1, TM, D), lambda b, i: (b, i, 0)), pl.BlockSpec((1, D), lambda b, i: (0, 0))],
        out_specs=pl.BlockSpec((1, TM, D), lambda b, i: (b, i, 0)),
        out_shape=jax.ShapeDtypeStruct((B, S, D), F32),
        compiler_params=_cp("parallel", "parallel"),
        name="final_norm",
    )(x3, gain.reshape(1, D))


def kernel(x, c, ctx, c_ctx, w_ada, b_ada, norm1, norm2, ab_w_in, ab_sink, ab_q_norm, ab_w_uq, ab_kv_norm,
           ab_w_ukv, ab_w_out, cd_w_in, cd_q_norm, cd_k_norm, cd_rpb, cd_w_out, w_router, w_gate, w_up, w_down,
           final_norm):
    c8 = jnp.concatenate([c, c_ctx[None, :], jnp.zeros((8 - B - 1, D), F32)], axis=0)
    modt = ada_modulation(c8, w_ada, b_ada)
    c64, s64 = _rope_tables(64)
    zeros64 = jnp.zeros((T, 64), F32)
    tabs = {
        "r128": _rope_tables(HD),
        "r64x2": (jnp.concatenate([c64, c64], axis=1), jnp.concatenate([s64, s64], axis=1)),
        "r64pad": (jnp.concatenate([c64, zeros64], axis=1), jnp.concatenate([s64, zeros64], axis=1)),
    }
    x2d = jnp.concatenate([x, ctx], axis=1).reshape(R, D)
    for l in range(DEPTH):
        need_ctx = l < DEPTH - 1
        i = l // 2
        h1 = norm_modulate(x2d, norm1[l], modt, l, 1, 0)
        if l % 2 == 0:
            att = mixer_ab(h1, need_ctx, ab_w_in[i], ab_sink[i], ab_q_norm[i], ab_w_uq[i], ab_kv_norm[i],
                           ab_w_ukv[i], tabs)
            w_out = ab_w_out[i]
        else:
            att = mixer_cd(h1, need_ctx, cd_w_in[i], cd_q_norm[i], cd_k_norm[i], cd_rpb[i], tabs)
            w_out = cd_w_out[i]
        x2d = matmul_gated_residual(att, w_out.astype(BF16), x2d, modt, l, 2)
        x2d = moe_block(x2d, norm2[l], modt, l, w_router[l].T, w_gate, w_up, w_down)
    return final_rmsnorm(x2d.reshape(B, T, D), final_norm)
```

```python
import functools

import jax
import jax.numpy as jnp
from jax import lax
from jax.experimental import pallas as pl
from jax.experimental.pallas import tpu as pltpu

F32 = jnp.float32
BF16 = jnp.bfloat16
I32 = jnp.int32

D = 2048
B = 4
S = 2048
L = 256
T = S + L
R = B * T
DEPTH = 4
GRID_W = 64
HD = 128
ROPE_BASE = 10000.0
EPS = 1e-6
NEG = -1e30
LOG2E = 1.4426950408889634
QK_SCALE_LOG2 = HD ** -0.5 * LOG2E
MLA_SCALE_LOG2 = 192 ** -0.5 * LOG2E
WINDOW = 128
BQ = 128
NB_ROWS = 8
NB_COLS = 16
E = 16
CAP_L = 2 * S // E
CAP_C = 2 * L // E
SLOTS = CAP_L + CAP_C
FE = 1024
FFN_ROWS = 384
MLA_Q = 512
MLA_KV = 256
AB_IN = 2368
CD_IN = 4608

ATTN_SUB = 128
TM = 256
TILES = T // TM
LAT_TILES = S // TM
CTX_ROW = B
VMEM_LIMIT = 56 * 1024 * 1024


def _cp(*sem):
    return pltpu.CompilerParams(dimension_semantics=sem, vmem_limit_bytes=VMEM_LIMIT)


def _dot(a, b):
    return jnp.dot(a, b, preferred_element_type=F32)


def _dot_nt(a, b):
    return lax.dot_general(a, b, (((1,), (1,)), ((), ())), preferred_element_type=F32)


def _ada_body(c_ref, w_ref, b_ref, o_ref):
    c = c_ref[...]
    a = c * jax.nn.sigmoid(c)
    o_ref[0] = _dot(a.astype(BF16), w_ref[0].astype(BF16)) + b_ref[0]


def ada_modulation(c8, w_ada, b_ada):
    tn = 1024
    n = 6 * D
    return pl.pallas_call(
        _ada_body,
        grid=(DEPTH, n // tn),
        in_specs=[
            pl.BlockSpec((8, D), lambda l, j: (0, 0)),
            pl.BlockSpec((1, D, tn), lambda l, j: (l, 0, j)),
            pl.BlockSpec((1, 1, tn), lambda l, j: (l, 0, j)),
        ],
        out_specs=pl.BlockSpec((1, 8, tn), lambda l, j: (l, 0, j)),
        out_shape=jax.ShapeDtypeStruct((DEPTH, 8, n), F32),
        compiler_params=_cp("parallel", "parallel"),
        name="ada_modulation",
    )(c8, w_ada, b_ada.reshape(DEPTH, 1, n))


def _rms(x, gain):
    return x * lax.rsqrt(jnp.mean(x * x, axis=-1, keepdims=True) + EPS) * gain


TM_IN = T // 6
TM_BIG = T // 3


def _row_mod(g_ref, tile, tm):
    per = T // tm
    r = (tile % per) * tm + lax.broadcasted_iota(I32, (tm, 1), 0)
    return jnp.where(r >= S, g_ref[0, CTX_ROW:CTX_ROW + 1, :], g_ref[0, pl.ds(tile // per, 1), :])


def _proj_in_body(x_ref, gain_ref, sc_ref, sh_ref, w_ref, o_ref):
    tile = pl.program_id(1)
    h = _rms(x_ref[...], gain_ref[...]) * (1.0 + _row_mod(sc_ref, tile, TM_IN)) + _row_mod(sh_ref, tile, TM_IN)
    o_ref[...] = _dot(h.astype(BF16), w_ref[...])


def project_in(x2d, gain, modt, l, w, tn):
    n = w.shape[1]
    return pl.pallas_call(
        _proj_in_body,
        grid=(n // tn, R // TM_IN),
        in_specs=[
            pl.BlockSpec((TM_IN, D), lambda j, i: (i, 0)),
            pl.BlockSpec((1, D), lambda j, i: (0, 0)),
            pl.BlockSpec((1, 8, D), lambda j, i: (l, 0, 1)),
            pl.BlockSpec((1, 8, D), lambda j, i: (l, 0, 0)),
            pl.BlockSpec((D, tn), lambda j, i: (0, j)),
        ],
        out_specs=pl.BlockSpec((TM_IN, tn), lambda j, i: (i, j)),
        out_shape=jax.ShapeDtypeStruct((R, n), F32),
        compiler_params=_cp("parallel", "parallel"),
        name="project_in",
    )(x2d, gain.reshape(1, D), modt, modt, w)


def _mm_res_body(a_ref, b_ref, x_ref, g_ref, o_ref):
    y = _dot(a_ref[...], b_ref[...])
    o_ref[...] = x_ref[...] + _row_mod(g_ref, pl.program_id(1), TM_BIG) * y


def matmul_gated_residual(a, b, x2d, modt, l, chunk):
    tn = 1024
    k = a.shape[1]
    per = D // tn
    return pl.pallas_call(
        _mm_res_body,
        grid=(D // tn, R // TM_BIG),
        in_specs=[
            pl.BlockSpec((TM_BIG, k), lambda j, i: (i, 0)),
            pl.BlockSpec((k, tn), lambda j, i: (0, j)),
            pl.BlockSpec((TM_BIG, tn), lambda j, i: (i, j)),
            pl.BlockSpec((1, 8, tn), lambda j, i: (l, 0, chunk * per + j)),
        ],
        out_specs=pl.BlockSpec((TM_BIG, tn), lambda j, i: (i, j)),
        out_shape=jax.ShapeDtypeStruct((R, D), F32),
        compiler_params=_cp("parallel", "parallel"),
        name="matmul_gated_residual",
    )(a, b, x2d, modt)


def _softmax_pv(parts, sink, rows):
    m = parts[0][0].max(axis=-1, keepdims=True)
    for s, _ in parts[1:]:
        m = jnp.maximum(m, s.max(axis=-1, keepdims=True))
    if sink is not None:
        m = jnp.maximum(m, sink)
    den = jnp.zeros((rows, 1), F32) if sink is None else jnp.exp2(sink - m)
    acc = None
    for s, v in parts:
        p = jnp.exp2(s - m)
        den = den + p.sum(axis=-1, keepdims=True)
        pv = _dot(p.astype(BF16), v)
        acc = pv if acc is None else acc + pv
    return acc / den


def _dense_attn_body(*refs, g, tq, dk, dv, has_sink):
    if has_sink:
        q_ref, k_ref, v_ref, sink_ref, _, o_ref = refs
        sink = sink_ref[0]
    else:
        q_ref, k_ref, v_ref, _, o_ref = refs
        sink = None
    sub = min(tq, ATTN_SUB) if g > 1 else tq
    for j in range(g):
        for r in range(tq // sub):
            rows = slice(r * sub, (r + 1) * sub)
            s = _dot_nt(q_ref[0, rows, j * dk:(j + 1) * dk], k_ref[0])
            sk = None if sink is None else sink[j * tq + r * sub:j * tq + (r + 1) * sub]
            o = _softmax_pv([(s, v_ref[0])], sk, sub)
            o_ref[0, rows, j * dv:(j + 1) * dv] = o.astype(o_ref.dtype)


def dense_attention(att, q, k, v, *, kvh, g, dk, dv, tq, q_off, n_q, nk, k_blk, col_off, sink_col=None):
    in_specs = [
        pl.BlockSpec((1, tq, g * dk), lambda b, h, i: (b, i + q_off, h)),
        pl.BlockSpec((1, nk, dk), lambda b, h, i: (b, k_blk, h)),
        pl.BlockSpec((1, nk, dv), lambda b, h, i: (b, k_blk, h)),
    ]
    args = [q, k, v]
    if sink_col is not None:
        in_specs.append(pl.BlockSpec((1, g * tq, 1), lambda b, h, i: (h, 0, 0)))
        args.append(sink_col)
    in_specs.append(pl.BlockSpec(memory_space=pl.ANY))
    args.append(att)
    return pl.pallas_call(
        functools.partial(_dense_attn_body, g=g, tq=tq, dk=dk, dv=dv, has_sink=sink_col is not None),
        grid=(B, kvh, n_q),
        in_specs=in_specs,
        out_specs=pl.BlockSpec((1, tq, g * dv), lambda b, h, i: (b, i + q_off, h + col_off)),
        out_shape=jax.ShapeDtypeStruct(att.shape, att.dtype),
        input_output_aliases={len(args) - 1: 0},
        compiler_params=_cp("parallel", "parallel", "parallel"),
        name="dense_attention",
    )(*args)


WIN_TQ = 512
WIN_SPAN = WIN_TQ + 2 * WINDOW


def _window_attn_body(q_ref, k_ref, v_ref, sink_ref, _, o_ref, *, g):
    q0 = pl.program_id(2) * WIN_TQ
    start = pl.multiple_of(jnp.clip(q0 - WINDOW, 0, S - WIN_SPAN), WINDOW)
    kl = k_ref[0, pl.ds(start, WIN_SPAN), :]
    vl = v_ref[0, pl.ds(start, WIN_SPAN), :]
    qi = lax.broadcasted_iota(I32, (WIN_TQ, WIN_SPAN), 0) + q0
    kj = lax.broadcasted_iota(I32, (WIN_TQ, WIN_SPAN), 1) + start
    valid = jnp.abs(qi - kj) <= WINDOW
    for j in range(g):
        q = q_ref[0, :, j * HD:(j + 1) * HD]
        s_loc = jnp.where(valid, _dot_nt(q, kl), NEG)
        s_ctx = _dot_nt(q, k_ref[0, S:T, :])
        o = _softmax_pv([(s_loc, vl), (s_ctx, v_ref[0, S:T, :])], sink_ref[0, j * WIN_TQ:(j + 1) * WIN_TQ],
                        WIN_TQ)
        o_ref[0, :, j * HD:(j + 1) * HD] = o.astype(o_ref.dtype)


def window_attention(att, q, k, v, sink_col, *, kvh, g):
    tq = WIN_TQ
    return pl.pallas_call(
        functools.partial(_window_attn_body, g=g),
        grid=(B, kvh, S // tq),
        in_specs=[
            pl.BlockSpec((1, tq, g * HD), lambda b, h, m: (b, m, h)),
            pl.BlockSpec((1, T, HD), lambda b, h, m: (b, 0, h)),
            pl.BlockSpec((1, T, HD), lambda b, h, m: (b, 0, h)),
            pl.BlockSpec((1, g * WIN_TQ, 1), lambda b, h, m: (h, 0, 0)),
            pl.BlockSpec(memory_space=pl.ANY),
        ],
        out_specs=pl.BlockSpec((1, tq, g * HD), lambda b, h, m: (b, m, h)),
        out_shape=jax.ShapeDtypeStruct(att.shape, att.dtype),
        input_output_aliases={4: 0},
        compiler_params=_cp("parallel", "parallel", "parallel"),
        name="window_attention",
    )(q, k, v, sink_col, att)


ROWS_N = S // GRID_W
NBR_G = 4
NBR_KR = NBR_G + NB_ROWS
NBR_GROUPS = ROWS_N // NBR_G
NBR_PATTERN_GROUPS = (0, 1, NBR_GROUPS - 1)


def _nbr_window_start(grp):
    return NBR_G * grp - NB_ROWS // 2


def _nbr_bias_body(rpb_ref, o_ref):
    h = pl.program_id(0)
    n_dc = 2 * NB_COLS - 1
    n_dr = 2 * NB_ROWS - 1
    qc = lax.broadcasted_iota(I32, (GRID_W, GRID_W), 0)
    kc = lax.broadcasted_iota(I32, (GRID_W, GRID_W), 1)
    dc = jnp.clip(kc - qc + NB_COLS - 1, 0, n_dc - 1)
    cs = jnp.clip(qc - NB_COLS // 2, 0, GRID_W - NB_COLS)
    col_ok = (kc >= cs) & (kc < cs + NB_COLS)
    tiles = [jnp.zeros((GRID_W, GRID_W), F32) for _ in range(n_dr)]
    for t in range(n_dc):
        hit = dc == t
        for dr in range(n_dr):
            tiles[dr] = jnp.where(hit, rpb_ref[h, dr * n_dc + t] * LOG2E, tiles[dr])
    tiles = [jnp.where(col_ok, x, NEG) for x in tiles]
    masked = jnp.full((GRID_W, GRID_W), NEG, F32)
    for p, grp in enumerate(NBR_PATTERN_GROUPS):
        start = min(max(_nbr_window_start(grp), 0), ROWS_N - NBR_KR)
        for ri in range(NBR_G):
            r = NBR_G * grp + ri
            rs = min(max(r - NB_ROWS // 2, 0), ROWS_N - NB_ROWS)
            row = [tiles[start + kr - r + NB_ROWS - 1] if rs <= start + kr < rs + NB_ROWS else masked
                   for kr in range(NBR_KR)]
            o_ref[0, p, ri * GRID_W:(ri + 1) * GRID_W, :] = jnp.concatenate(row, axis=1)


def nbr_bias_table(rpb):
    heads = rpb.shape[0]
    return pl.pallas_call(
        _nbr_bias_body,
        grid=(heads,),
        in_specs=[pl.BlockSpec(memory_space=pltpu.SMEM)],
        out_specs=pl.BlockSpec((1, 3, NBR_G * GRID_W, NBR_KR * GRID_W), lambda h: (h, 0, 0, 0)),
        out_shape=jax.ShapeDtypeStruct((heads, 3, NBR_G * GRID_W, NBR_KR * GRID_W), F32),
        compiler_params=_cp("parallel"),
        name="nbr_bias_table",
    )(rpb.astype(F32).reshape(heads, -1))


NBR_STEP = 2


def _nbr_attn_body(q_ref, k_ref, v_ref, bias_ref, _, o_ref):
    n_loc = NBR_KR * GRID_W
    rows_g = NBR_G * GRID_W
    for w in range(NBR_STEP):
        grp = pl.program_id(2) * NBR_STEP + w
        pattern = jnp.where(grp == 0, 0, jnp.where(grp == NBR_GROUPS - 1, 2, 1))
        start_row = jnp.clip(_nbr_window_start(grp), 0, ROWS_N - NBR_KR)
        start = pl.multiple_of(start_row * GRID_W, GRID_W)
        rows = slice(w * rows_g, (w + 1) * rows_g)
        q = q_ref[0, rows, :]
        s_loc = _dot_nt(q, k_ref[0, pl.ds(start, n_loc), :]) + bias_ref[0, pattern]
        s_ctx = _dot_nt(q, k_ref[0, S:T, :])
        o = _softmax_pv([(s_loc, v_ref[0, pl.ds(start, n_loc), :]), (s_ctx, v_ref[0, S:T, :])], None, rows_g)
        o_ref[0, rows, :] = o.astype(o_ref.dtype)


def neighbourhood_attention(att, q, k, v, bias_tab, *, heads, col_off):
    tq = NBR_STEP * NBR_G * GRID_W
    return pl.pallas_call(
        _nbr_attn_body,
        grid=(B, heads, NBR_GROUPS // NBR_STEP),
        in_specs=[
            pl.BlockSpec((1, tq, HD), lambda b, h, r: (b, r, h)),
            pl.BlockSpec((1, T, HD), lambda b, h, r: (b, 0, h)),
            pl.BlockSpec((1, T, HD), lambda b, h, r: (b, 0, h)),
            pl.BlockSpec((1, 3, NBR_G * GRID_W, NBR_KR * GRID_W), lambda b, h, r: (h, 0, 0, 0)),
            pl.BlockSpec(memory_space=pl.ANY),
        ],
        out_specs=pl.BlockSpec((1, tq, HD), lambda b, h, r: (b, r, h + col_off)),
        out_shape=jax.ShapeDtypeStruct(att.shape, att.dtype),
        input_output_aliases={4: 0},
        compiler_params=_cp("parallel", "parallel", "parallel"),
        name="neighbourhood_attention",
    )(q, k, v, bias_tab, att)


def _cumsum_lanes(mask, tri):
    n = mask.shape[1]
    carry = jnp.zeros((mask.shape[0], 1), F32)
    out = []
    for j in range(n // 256):
        blk = jnp.where(mask[:, j * 256:(j + 1) * 256], 1.0, 0.0).astype(BF16)
        c = _dot(blk, tri) + carry
        carry = c[:, 255:256]
        out.append(c)
    return jnp.concatenate(out, axis=1) if len(out) > 1 else out[0]


def _count(mask):
    return jnp.sum(jnp.where(mask, 1.0, 0.0), axis=1, keepdims=True)


def _topk_slots(affs, caps, tri):
    rows = affs[0].shape[0]

    def step(carry):
        bounds, _ = carry
        new, moved = [], []
        for aff, cap, (lo, hi) in zip(affs, caps, bounds):
            mid = 0.5 * (lo + hi)
            ok = _count(aff >= mid) >= cap
            nlo = jnp.where(ok, mid, lo)
            nhi = jnp.where(ok, hi, mid)
            new.append((nlo, nhi))
            moved.append(jnp.where((nlo != lo) | (nhi != hi), 1.0, 0.0))
        return tuple(new), jnp.max(functools.reduce(jnp.maximum, moved))

    start = tuple((jnp.zeros((rows, 1), F32), jnp.full((rows, 1), 2.0, F32)) for _ in affs)
    bounds, _ = lax.while_loop(lambda c: c[1] > 0.0, step, (start, jnp.float32(1.0)))
    slots = []
    for aff, cap, (lo, hi) in zip(affs, caps, bounds):
        gt = aff >= hi
        eq = (aff >= lo) & (aff < hi)
        need = cap - _count(gt)
        sel = gt | (eq & (_cumsum_lanes(eq, tri) <= need))
        pos = _cumsum_lanes(sel, tri) - 1.0
        slots.append(jnp.where(sel, pos.astype(I32), -1))
    return slots


def _router_body(x_ref, gain_ref, sc_ref, sh_ref, wr_ref, h_ref, slot_ref, aff_ref, lg_scr):
    b = pl.program_id(0)
    tt = pl.program_id(1)
    row = jnp.where(tt == LAT_TILES, CTX_ROW, b)
    y = _rms(x_ref[...], gain_ref[...])
    h = y * (1.0 + sc_ref[0, pl.ds(row, 1), :]) + sh_ref[0, pl.ds(row, 1), :]
    h_ref[...] = h.astype(BF16)
    w = wr_ref[...]
    w_hi, h_hi = w.astype(BF16), h.astype(BF16)
    w_lo = (w - w_hi.astype(F32)).astype(BF16)
    h_lo = (h - h_hi.astype(F32)).astype(BF16)
    lg_scr[tt] = _dot_nt(w_hi, h_hi) + (_dot_nt(w_hi, h_lo) + _dot_nt(w_lo, h_hi))

    @pl.when(tt == TILES - 1)
    def _():
        tri = jnp.where(lax.broadcasted_iota(I32, (256, 256), 0) <= lax.broadcasted_iota(I32, (256, 256), 1),
                        1.0, 0.0).astype(BF16)
        lat = jnp.concatenate([lg_scr[j] for j in range(LAT_TILES)], axis=1)
        affs = []
        for lg in (lat, lg_scr[LAT_TILES]):
            ex = jnp.exp(lg - lg.max(axis=0, keepdims=True))
            affs.append(ex / ex.sum(axis=0, keepdims=True))
        slots = _topk_slots(affs, (CAP_L, CAP_C), tri)
        for (off, n, _, _), aff, slot in zip(_SETS, affs, slots):
            slot_ref[0, :, off:off + n] = slot
            aff_ref[0, :, off:off + n] = aff


def router(x2d, gain, modt, l, w_router_t):
    return pl.pallas_call(
        _router_body,
        grid=(B, TILES),
        in_specs=[
            pl.BlockSpec((TM, D), lambda b, t: (b * TILES + t, 0)),
            pl.BlockSpec((1, D), lambda b, t: (0, 0)),
            pl.BlockSpec((1, 8, D), lambda b, t: (l, 0, 4)),
            pl.BlockSpec((1, 8, D), lambda b, t: (l, 0, 3)),
            pl.BlockSpec((E, D), lambda b, t: (0, 0)),
        ],
        out_specs=[
            pl.BlockSpec((TM, D), lambda b, t: (b * TILES + t, 0)),
            pl.BlockSpec((1, E, T), lambda b, t: (b, 0, 0)),
            pl.BlockSpec((1, E, T), lambda b, t: (b, 0, 0)),
        ],
        out_shape=[
            jax.ShapeDtypeStruct((R, D), BF16),
            jax.ShapeDtypeStruct((B, E, T), I32),
            jax.ShapeDtypeStruct((B, E, T), F32),
        ],
        scratch_shapes=[pltpu.VMEM((TILES, E, TM), F32)],
        compiler_params=_cp("parallel", "arbitrary"),
        name="router",
    )(x2d, gain.reshape(1, D), modt, modt, w_router_t)


_SETS = ((0, S, CAP_L, 0), (S, L, CAP_C, CAP_L))


def _dispatch_body(h_ref, slot_ref, aff_ref, xin_ref, gate_ref):
    e = pl.program_id(1)
    srow = slot_ref[0, pl.ds(e, 1), :]
    arow = aff_ref[0, pl.ds(e, 1), :]
    for off, n, cap, o0 in _SETS:
        hit = lax.broadcasted_iota(I32, (cap, n), 0) == srow[:, off:off + n]
        onehot = jnp.where(hit, 1.0, 0.0).astype(BF16)
        xin_ref[0, 0, o0:o0 + cap, :] = _dot(onehot, h_ref[0, off:off + n, :]).astype(BF16)
        gate_ref[0, 0, o0:o0 + cap, :] = jnp.sum(jnp.where(hit, arow[:, off:off + n], 0.0), axis=1, keepdims=True)


def dispatch(h2, slot, aff):
    return pl.pallas_call(
        _dispatch_body,
        grid=(B, E),
        in_specs=[
            pl.BlockSpec((1, T, D), lambda b, e: (b, 0, 0)),
            pl.BlockSpec((1, E, T), lambda b, e: (b, 0, 0)),
            pl.BlockSpec((1, E, T), lambda b, e: (b, 0, 0)),
        ],
        out_specs=[
            pl.BlockSpec((1, 1, SLOTS, D), lambda b, e: (e, b, 0, 0)),
            pl.BlockSpec((1, 1, SLOTS, 1), lambda b, e: (e, b, 0, 0)),
        ],
        out_shape=[
            jax.ShapeDtypeStruct((E, B, SLOTS, D), BF16),
            jax.ShapeDtypeStruct((E, B, SLOTS, 1), F32),
        ],
        compiler_params=_cp("parallel", "parallel"),
        name="moe_dispatch",
    )(h2.reshape(B, T, D), slot, aff)


def _ffn_body(x_ref, wg_ref, wu_ref, wd_ref, gate_ref, o_ref, acc_ref):
    f = pl.program_id(1)
    wg = wg_ref[0, 0].astype(BF16)
    wu = wu_ref[0, 0].astype(BF16)
    wd = wd_ref[0, 0].astype(BF16)

    @pl.when(f == 0)
    def _():
        acc_ref[...] = jnp.zeros_like(acc_ref)

    for r in range(x_ref.shape[1] // FFN_ROWS):
        rows = slice(r * FFN_ROWS, (r + 1) * FFN_ROWS)
        x = x_ref[0, rows, :]
        a = _dot(x, wg)
        u = _dot(x, wu)
        hid = (a * jax.nn.sigmoid(a) * u).astype(BF16)
        acc_ref[rows, :] += _dot(hid, wd)

    @pl.when(f == pl.num_programs(1) - 1)
    def _():
        o_ref[0] = (acc_ref[...] * gate_ref[0]).astype(o_ref.dtype)


def expert_ffn(xin, gates, w_gate, w_up, w_down, l):
    tf = 256
    m = B * SLOTS
    return pl.pallas_call(
        _ffn_body,
        grid=(E, FE // tf),
        in_specs=[
            pl.BlockSpec((1, m, D), lambda e, f: (e, 0, 0)),
            pl.BlockSpec((1, 1, D, tf), lambda e, f: (l, e, 0, f)),
            pl.BlockSpec((1, 1, D, tf), lambda e, f: (l, e, 0, f)),
            pl.BlockSpec((1, 1, tf, D), lambda e, f: (l, e, f, 0)),
            pl.BlockSpec((1, m, 1), lambda e, f: (e, 0, 0)),
        ],
        out_specs=pl.BlockSpec((1, m, D), lambda e, f: (e, 0, 0)),
        out_shape=jax.ShapeDtypeStruct((E, m, D), BF16),
        scratch_shapes=[pltpu.VMEM((m, D), F32)],
        compiler_params=_cp("parallel", "arbitrary"),
        name="expert_ffn",
    )(xin.reshape(E, m, D), w_gate, w_up, w_down, gates.reshape(E, m, 1))


def _combine_body(o_ref, slot_ref, x_ref, g_ref, out_ref):
    b = pl.program_id(0)
    tt = pl.program_id(2)
    tn = out_ref.shape[1]

    def run(cap, o0, row):
        acc = jnp.zeros((TM, tn), F32)
        sl = lax.broadcasted_iota(I32, (TM, cap), 1)
        for e in range(E):
            onehot = jnp.where(slot_ref[0, :, e:e + 1] == sl, 1.0, 0.0).astype(BF16)
            acc = acc + _dot(onehot, o_ref[e, 0, o0:o0 + cap, :])
        out_ref[...] = x_ref[...] + g_ref[0, pl.ds(row, 1), :] * acc

    @pl.when(tt < LAT_TILES)
    def _():
        run(CAP_L, 0, b)

    @pl.when(tt == LAT_TILES)
    def _():
        run(CAP_C, CAP_L, CTX_ROW)


def combine(o, slot_tok, x2d, modt, l):
    tn = 1024
    per = D // tn
    return pl.pallas_call(
        _combine_body,
        grid=(B, per, TILES),
        in_specs=[
            pl.BlockSpec((E, 1, SLOTS, tn), lambda b, j, t: (0, b, 0, j)),
            pl.BlockSpec((1, TM, E), lambda b, j, t: (b, t, 0)),
            pl.BlockSpec((TM, tn), lambda b, j, t: (b * TILES + t, j)),
            pl.BlockSpec((1, 8, tn), lambda b, j, t: (l, 0, 5 * per + j)),
        ],
        out_specs=pl.BlockSpec((TM, tn), lambda b, j, t: (b * TILES + t, j)),
        out_shape=jax.ShapeDtypeStruct((R, D), F32),
        compiler_params=_cp("parallel", "parallel", "parallel"),
        name="moe_combine",
    )(o.reshape(E, B, SLOTS, D), slot_tok, x2d, modt)


def moe_block(x2d, gain, modt, l, w_router_t, w_gate, w_up, w_down):
    h2, slot, aff = router(x2d, gain, modt, l, w_router_t)
    xin, gates = dispatch(h2, slot, aff)
    o = expert_ffn(xin, gates, w_gate, w_up, w_down, l)
    return combine(o, jnp.swapaxes(slot, 1, 2), x2d, modt, l)


def _rope_tables(d):
    half = d // 2
    t = jnp.arange(S)
    inv = ROPE_BASE ** (-jnp.arange(0, half, 2, dtype=F32) / half)
    parts_c, parts_s = [], []
    for pos in (t // GRID_W, t % GRID_W):
        ang = pos.astype(F32)[:, None] * inv[None, :]
        parts_c += [jnp.cos(ang), jnp.cos(ang)]
        parts_s += [-jnp.sin(ang), jnp.sin(ang)]
    cos = jnp.concatenate([jnp.concatenate(parts_c, axis=1), jnp.ones((L, d), F32)], axis=0)
    sin = jnp.concatenate([jnp.concatenate(parts_s, axis=1), jnp.zeros((L, d), F32)], axis=0)
    return cos, sin


def _rope_lanes(x, cos, sin, quarter):
    lane = lax.broadcasted_iota(I32, x.shape, 1)
    sw = jnp.where(lane % (2 * quarter) < quarter, pltpu.roll(x, HD - quarter, 1), pltpu.roll(x, quarter, 1))
    return x * cos + sw * sin


def _table_spec():
    return pl.BlockSpec((TM, HD), lambda i: (i % TILES, 0))


def _post_ab_body(p_ref, c128_ref, s128_ref, c64_ref, s64_ref, qn_ref, kvn_ref,
                  aq_ref, ak_ref, av_ref, cq_ref, ckv_ref, kr_ref):
    cos, sin = c128_ref[...], s128_ref[...]
    for h in range(8):
        aq = _rope_lanes(p_ref[:, h * HD:(h + 1) * HD], cos, sin, 32) * QK_SCALE_LOG2
        aq_ref[:, h * HD:(h + 1) * HD] = aq.astype(BF16)
    for h in range(2):
        ak_ref[:, h * HD:(h + 1) * HD] = _rope_lanes(p_ref[:, 1024 + h * HD:1024 + (h + 1) * HD], cos, sin,
                                                     32).astype(BF16)
    av_ref[...] = p_ref[:, 1280:1536].astype(BF16)
    cq_ref[...] = _rms(p_ref[:, 1536:2048], qn_ref[...]).astype(BF16)
    ckv_ref[...] = _rms(p_ref[:, 2048:2304], kvn_ref[...]).astype(BF16)
    kr = jnp.concatenate([p_ref[:, 2304:2368], jnp.zeros((TM, 64), F32)], axis=1)
    kr_ref[...] = _rope_lanes(kr, c64_ref[...], s64_ref[...], 16).astype(BF16)


def post_project_ab(proj, tabs128, tabs64pad, q_norm, kv_norm):
    widths = (1024, 256, 256, MLA_Q, MLA_KV, HD)
    return pl.pallas_call(
        _post_ab_body,
        grid=(R // TM,),
        in_specs=[pl.BlockSpec((TM, AB_IN), lambda i: (i, 0))] + [_table_spec()] * 4
        + [pl.BlockSpec((1, MLA_Q), lambda i: (0, 0)), pl.BlockSpec((1, MLA_KV), lambda i: (0, 0))],
        out_specs=[pl.BlockSpec((TM, w), lambda i: (i, 0)) for w in widths],
        out_shape=[jax.ShapeDtypeStruct((R, w), BF16) for w in widths],
        compiler_params=_cp("parallel"),
        name="post_project_ab",
    )(proj, *tabs128, *tabs64pad, q_norm.reshape(1, MLA_Q), kv_norm.reshape(1, MLA_KV))


def _q_up_body(cq_ref, w_ref, c64_ref, s64_ref, o_ref):
    y = _dot(cq_ref[...], w_ref[...])
    cos, sin = c64_ref[...], s64_ref[...]
    zeros = jnp.zeros((TM, 64), F32)
    for j in range(4):
        rot = _rope_lanes(y[:, 1024 + j * HD:1024 + (j + 1) * HD], cos, sin, 16)
        for u in range(2):
            h = 2 * j + u
            o_ref[:, 256 * h:256 * h + HD] = (y[:, h * HD:(h + 1) * HD] * MLA_SCALE_LOG2).astype(BF16)
            o_ref[:, 256 * h + HD:256 * (h + 1)] = jnp.concatenate(
                [rot[:, 64 * u:64 * (u + 1)] * MLA_SCALE_LOG2, zeros], axis=1).astype(BF16)


def mla_q_up(cq, w_uq_split, tabs64x2):
    return pl.pallas_call(
        _q_up_body,
        grid=(R // TM,),
        in_specs=[pl.BlockSpec((TM, MLA_Q), lambda i: (i, 0)), pl.BlockSpec((MLA_Q, 1536), lambda i: (0, 0)),
                  _table_spec(), _table_spec()],
        out_specs=pl.BlockSpec((TM, 2048), lambda i: (i, 0)),
        out_shape=jax.ShapeDtypeStruct((R, 2048), BF16),
        compiler_params=_cp("parallel"),
        name="mla_q_up",
    )(cq, w_uq_split, *tabs64x2)


def _kv_up_body(ckv_ref, w_ref, kr_ref, k_ref, v_ref):
    y = _dot(ckv_ref[...], w_ref[...])
    kr = kr_ref[...]
    for h in range(8):
        k_ref[:, 256 * h:256 * h + HD] = y[:, h * HD:(h + 1) * HD].astype(BF16)
        k_ref[:, 256 * h + HD:256 * (h + 1)] = kr
    v_ref[...] = y[:, 1024:].astype(BF16)


def mla_kv_up(ckv, w_ukv_split, kr):
    return pl.pallas_call(
        _kv_up_body,
        grid=(R // TM,),
        in_specs=[pl.BlockSpec((TM, MLA_KV), lambda i: (i, 0)), pl.BlockSpec((MLA_KV, 2048), lambda i: (0, 0)),
                  pl.BlockSpec((TM, HD), lambda i: (i, 0))],
        out_specs=[pl.BlockSpec((TM, 2048), lambda i: (i, 0)), pl.BlockSpec((TM, 1024), lambda i: (i, 0))],
        out_shape=[jax.ShapeDtypeStruct((R, 2048), BF16), jax.ShapeDtypeStruct((R, 1024), BF16)],
        compiler_params=_cp("parallel"),
        name="mla_kv_up",
    )(ckv, w_ukv_split, kr)


def _post_cd_body(p_ref, c128_ref, s128_ref, qn_ref, kn_ref, cq_ref, ck_ref, cv_ref, dq_ref, dk_ref, dv_ref):
    cos, sin = c128_ref[...], s128_ref[...]
    for h in range(8):
        x = _rms(p_ref[:, h * HD:(h + 1) * HD], qn_ref[...])
        cq_ref[:, h * HD:(h + 1) * HD] = (_rope_lanes(x, cos, sin, 32) * QK_SCALE_LOG2).astype(BF16)
    for h in range(2):
        x = _rms(p_ref[:, 1024 + h * HD:1024 + (h + 1) * HD], kn_ref[...])
        ck_ref[:, h * HD:(h + 1) * HD] = _rope_lanes(x, cos, sin, 32).astype(BF16)
    cv_ref[...] = p_ref[:, 1280:1536].astype(BF16)
    dq_ref[...] = (p_ref[:, 1536:2560] * QK_SCALE_LOG2).astype(BF16)
    dk_ref[...] = p_ref[:, 2560:3584].astype(BF16)
    dv_ref[...] = p_ref[:, 3584:4608].astype(BF16)


def post_project_cd(proj, tabs128, q_norm, k_norm):
    widths = (1024, 256, 256, 1024, 1024, 1024)
    return pl.pallas_call(
        _post_cd_body,
        grid=(R // TM,),
        in_specs=[pl.BlockSpec((TM, CD_IN), lambda i: (i, 0)), _table_spec(), _table_spec(),
                  pl.BlockSpec((1, HD), lambda i: (0, 0)), pl.BlockSpec((1, HD), lambda i: (0, 0))],
        out_specs=[pl.BlockSpec((TM, w), lambda i: (i, 0)) for w in widths],
        out_shape=[jax.ShapeDtypeStruct((R, w), BF16) for w in widths],
        compiler_params=_cp("parallel"),
        name="post_project_cd",
    )(proj, *tabs128, q_norm.reshape(1, HD), k_norm.reshape(1, HD))


def _sink_col(sink, kvh, g, tq):
    col = (sink.astype(F32) * LOG2E).reshape(kvh, g, 1, 1)
    return jnp.broadcast_to(col, (kvh, g, tq, 1)).reshape(kvh, g * tq, 1)


def _split_heads(w, n_first):
    k = w.shape[0]
    w3 = w.reshape(k, 8, -1)
    return jnp.concatenate([w3[:, :, :n_first].reshape(k, -1), w3[:, :, n_first:].reshape(k, -1)], axis=1)


def mixer_ab(proj, need_ctx, sink, q_norm, w_uq, kv_norm, w_ukv, tabs):
    aq, ak, av, cq, ckv, kr = post_project_ab(proj, tabs["r128"], tabs["r64pad"], q_norm, kv_norm)
    bq = mla_q_up(cq, _split_heads(w_uq, 128).astype(BF16), tabs["r64x2"])
    bk, bv = mla_kv_up(ckv, _split_heads(w_ukv, 128).astype(BF16), kr)
    aq, ak, av, bq, bk, bv = (t.reshape(B, T, -1) for t in (aq, ak, av, bq, bk, bv))

    ctx_blk = S // L
    att = jnp.zeros((B, T, D), BF16)
    att = window_attention(att, aq, ak, av, _sink_col(sink, 2, 4, WIN_TQ), kvh=2, g=4)
    att = dense_attention(att, bq, bk, bv, kvh=8, g=1, dk=256, dv=HD, tq=512, q_off=0, n_q=S // 512, nk=T,
                          k_blk=0, col_off=8)
    if need_ctx:
        att = dense_attention(att, aq, ak, av, kvh=2, g=4, dk=HD, dv=HD, tq=L, q_off=ctx_blk, n_q=1, nk=L,
                              k_blk=ctx_blk, col_off=0, sink_col=_sink_col(sink, 2, 4, L))
        att = dense_attention(att, bq, bk, bv, kvh=8, g=1, dk=256, dv=HD, tq=L, q_off=ctx_blk, n_q=1, nk=L,
                              k_blk=ctx_blk, col_off=8)
    return att.reshape(R, D)


def mixer_cd(proj, need_ctx, q_norm, k_norm, rpb, tabs):
    cq, ck, cv, dq, dk, dv = (t.reshape(B, T, -1)
                              for t in post_project_cd(proj, tabs["r128"], q_norm, k_norm))
    ctx_blk = S // L
    att = jnp.zeros((B, T, D), BF16)
    att = dense_attention(att, cq, ck, cv, kvh=2, g=4, dk=HD, dv=HD, tq=BQ, q_off=0, n_q=S // BQ, nk=T, k_blk=0,
                          col_off=0)
    att = neighbourhood_attention(att, dq, dk, dv, nbr_bias_table(rpb), heads=8, col_off=8)
    if need_ctx:
        att = dense_attention(att, cq, ck, cv, kvh=2, g=4, dk=HD, dv=HD, tq=L, q_off=ctx_blk, n_q=1, nk=L,
                              k_blk=ctx_blk, col_off=0)
        att = dense_attention(att, dq, dk, dv, kvh=8, g=1, dk=HD, dv=HD, tq=L, q_off=ctx_blk, n_q=1, nk=L,
                              k_blk=ctx_blk, col_off=8)
    return att.reshape(R, D)


def _final_norm_body(x_ref, g_ref, o_ref):
    o_ref[0] = _rms(x_ref[0], g_ref[...])


def final_rmsnorm(x3, gain):
    return pl.pallas_call(
        _final_norm_body,
        grid=(B, S // TM),
        in_specs=[pl.BlockSpec((1, TM, D), lambda b, i: (b, i, 0)), pl.BlockSpec((1, D), lambda b, i: (0, 0))],
        out_specs=pl.BlockSpec((1, TM, D), lambda b, i: (b, i, 0)),
        out_shape=jax.ShapeDtypeStruct((B, S, D), F32),
        compiler_params=_cp("parallel", "parallel"),
        name="final_norm",
    )(x3, gain.reshape(1, D))


def kernel(x, c, ctx, c_ctx, w_ada, b_ada, norm1, norm2, ab_w_in, ab_sink, ab_q_norm, ab_w_uq, ab_kv_norm,
           ab_w_ukv, ab_w_out, cd_w_in, cd_q_norm, cd_k_norm, cd_rpb, cd_w_out, w_router, w_gate, w_up, w_down,
           final_norm):
    c8 = jnp.concatenate([c, c_ctx[None, :], jnp.zeros((8 - B - 1, D), F32)], axis=0)
    modt = ada_modulation(c8, w_ada, b_ada)
    c64, s64 = _rope_tables(64)
    zeros64 = jnp.zeros((T, 64), F32)
    tabs = {
        "r128": _rope_tables(HD),
        "r64x2": (jnp.concatenate([c64, c64], axis=1), jnp.concatenate([s64, s64], axis=1)),
        "r64pad": (jnp.concatenate([c64, zeros64], axis=1), jnp.concatenate([s64, zeros64], axis=1)),
    }
    x2d = jnp.concatenate([x, ctx], axis=1).reshape(R, D)
    for l in range(DEPTH):
        need_ctx = l < DEPTH - 1
        i = l // 2
        if l % 2 == 0:
            proj = project_in(x2d, norm1[l], modt, l, ab_w_in[i].astype(BF16), AB_IN)
            att = mixer_ab(proj, need_ctx, ab_sink[i], ab_q_norm[i], ab_w_uq[i], ab_kv_norm[i], ab_w_ukv[i], tabs)
            w_out = ab_w_out[i]
        else:
            proj = project_in(x2d, norm1[l], modt, l, cd_w_in[i].astype(BF16), CD_IN // 2)
            att = mixer_cd(proj, need_ctx, cd_q_norm[i], cd_k_norm[i], cd_rpb[i], tabs)
            w_out = cd_w_out[i]
        x2d = matmul_gated_residual(att, w_out.astype(BF16), x2d, modt, l, 2)
        x2d = moe_block(x2d, norm2[l], modt, l, w_router[l].T, w_gate, w_up, w_down)
    return final_rmsnorm(x2d.reshape(B, T, D), final_norm)
```

```python
import functools

import jax
import jax.numpy as jnp
from jax import lax
from jax.experimental import pallas as pl
from jax.experimental.pallas import tpu as pltpu

F32 = jnp.float32
BF16 = jnp.bfloat16
I32 = jnp.int32

D = 2048
B = 4
S = 2048
L = 256
T = S + L
R = B * T
DEPTH = 4
GRID_W = 64
HD = 128
ROPE_BASE = 10000.0
EPS = 1e-6
NEG = -1e30
LOG2E = 1.4426950408889634
QK_SCALE_LOG2 = HD ** -0.5 * LOG2E
MLA_SCALE_LOG2 = 192 ** -0.5 * LOG2E
WINDOW = 128
BQ = 128
NB_ROWS = 8
NB_COLS = 16
E = 16
CAP_L = 2 * S // E
CAP_C = 2 * L // E
SLOTS = CAP_L + CAP_C
FE = 1024
FFN_ROWS = 384
MLA_Q = 512
MLA_KV = 256
AB_IN = 2368
CD_IN = 4608

ATTN_SUB = 256
TM = 256
TILES = T // TM
LAT_TILES = S // TM
CTX_ROW = B
VMEM_LIMIT = 56 * 1024 * 1024


def _cp(*sem):
    return pltpu.CompilerParams(dimension_semantics=sem, vmem_limit_bytes=VMEM_LIMIT)


def _dot(a, b):
    return jnp.dot(a, b, preferred_element_type=F32)


def _dot_nt(a, b):
    return lax.dot_general(a, b, (((1,), (1,)), ((), ())), preferred_element_type=F32)


def _ada_body(c_ref, w_ref, b_ref, o_ref):
    c = c_ref[...]
    a = c * jax.nn.sigmoid(c)
    o_ref[0] = _dot(a.astype(BF16), w_ref[0].astype(BF16)) + b_ref[0]


def ada_modulation(c8, w_ada, b_ada):
    tn = 1024
    n = 6 * D
    return pl.pallas_call(
        _ada_body,
        grid=(DEPTH, n // tn),
        in_specs=[
            pl.BlockSpec((8, D), lambda l, j: (0, 0)),
            pl.BlockSpec((1, D, tn), lambda l, j: (l, 0, j)),
            pl.BlockSpec((1, 1, tn), lambda l, j: (l, 0, j)),
        ],
        out_specs=pl.BlockSpec((1, 8, tn), lambda l, j: (l, 0, j)),
        out_shape=jax.ShapeDtypeStruct((DEPTH, 8, n), F32),
        compiler_params=_cp("parallel", "parallel"),
        name="ada_modulation",
    )(c8, w_ada, b_ada.reshape(DEPTH, 1, n))


def _rms(x, gain):
    return x * lax.rsqrt(jnp.mean(x * x, axis=-1, keepdims=True) + EPS) * gain


TM_IN = T // 6
TM_BIG = T // 3


def _row_mod(g_ref, tile, tm):
    per = T // tm
    r = (tile % per) * tm + lax.broadcasted_iota(I32, (tm, 1), 0)
    return jnp.where(r >= S, g_ref[0, CTX_ROW:CTX_ROW + 1, :], g_ref[0, pl.ds(tile // per, 1), :])


def _proj_in_body(x_ref, gain_ref, sc_ref, sh_ref, w_ref, o_ref):
    tile = pl.program_id(1)
    h = _rms(x_ref[...], gain_ref[...]) * (1.0 + _row_mod(sc_ref, tile, TM_IN)) + _row_mod(sh_ref, tile, TM_IN)
    o_ref[...] = _dot(h.astype(BF16), w_ref[...])


def project_in(x2d, gain, modt, l, w, tn):
    n = w.shape[1]
    return pl.pallas_call(
        _proj_in_body,
        grid=(n // tn, R // TM_IN),
        in_specs=[
            pl.BlockSpec((TM_IN, D), lambda j, i: (i, 0)),
            pl.BlockSpec((1, D), lambda j, i: (0, 0)),
            pl.BlockSpec((1, 8, D), lambda j, i: (l, 0, 1)),
            pl.BlockSpec((1, 8, D), lambda j, i: (l, 0, 0)),
            pl.BlockSpec((D, tn), lambda j, i: (0, j)),
        ],
        out_specs=pl.BlockSpec((TM_IN, tn), lambda j, i: (i, j)),
        out_shape=jax.ShapeDtypeStruct((R, n), F32),
        compiler_params=_cp("parallel", "parallel"),
        name="project_in",
    )(x2d, gain.reshape(1, D), modt, modt, w)


def _mm_res_body(a_ref, b_ref, x_ref, g_ref, o_ref):
    y = _dot(a_ref[...], b_ref[...])
    o_ref[...] = x_ref[...] + _row_mod(g_ref, pl.program_id(1), TM_BIG) * y


def matmul_gated_residual(a, b, x2d, modt, l, chunk):
    tn = 1024
    k = a.shape[1]
    per = D // tn
    return pl.pallas_call(
        _mm_res_body,
        grid=(D // tn, R // TM_BIG),
        in_specs=[
            pl.BlockSpec((TM_BIG, k), lambda j, i: (i, 0)),
            pl.BlockSpec((k, tn), lambda j, i: (0, j)),
            pl.BlockSpec((TM_BIG, tn), lambda j, i: (i, j)),
            pl.BlockSpec((1, 8, tn), lambda j, i: (l, 0, chunk * per + j)),
        ],
        out_specs=pl.BlockSpec((TM_BIG, tn), lambda j, i: (i, j)),
        out_shape=jax.ShapeDtypeStruct((R, D), F32),
        compiler_params=_cp("parallel", "parallel"),
        name="matmul_gated_residual",
    )(a, b, x2d, modt)


def _softmax_pv(parts, sink, rows):
    m = parts[0][0].max(axis=-1, keepdims=True)
    for s, _ in parts[1:]:
        m = jnp.maximum(m, s.max(axis=-1, keepdims=True))
    if sink is not None:
        m = jnp.maximum(m, sink)
    den = jnp.zeros((rows, 1), F32) if sink is None else jnp.exp2(sink - m)
    acc = None
    for s, v in parts:
        p = jnp.exp2(s - m)
        den = den + p.sum(axis=-1, keepdims=True)
        pv = _dot(p.astype(BF16), v)
        acc = pv if acc is None else acc + pv
    return acc / den


def _dense_attn_body(*refs, g, tq, dk, dv, has_sink):
    if has_sink:
        q_ref, k_ref, v_ref, sink_ref, _, o_ref = refs
        sink = sink_ref[0]
    else:
        q_ref, k_ref, v_ref, _, o_ref = refs
        sink = None
    sub = min(tq, ATTN_SUB)
    for j in range(g):
        for r in range(tq // sub):
            rows = slice(r * sub, (r + 1) * sub)
            s = _dot_nt(q_ref[0, rows, j * dk:(j + 1) * dk], k_ref[0])
            sk = None if sink is None else sink[j * tq + r * sub:j * tq + (r + 1) * sub]
            o = _softmax_pv([(s, v_ref[0])], sk, sub)
            o_ref[0, rows, j * dv:(j + 1) * dv] = o.astype(o_ref.dtype)


def dense_attention(att, q, k, v, *, kvh, g, dk, dv, tq, q_off, n_q, nk, k_blk, col_off, sink_col=None):
    in_specs = [
        pl.BlockSpec((1, tq, g * dk), lambda b, h, i: (b, i + q_off, h)),
        pl.BlockSpec((1, nk, dk), lambda b, h, i: (b, k_blk, h)),
        pl.BlockSpec((1, nk, dv), lambda b, h, i: (b, k_blk, h)),
    ]
    args = [q, k, v]
    if sink_col is not None:
        in_specs.append(pl.BlockSpec((1, g * tq, 1), lambda b, h, i: (h, 0, 0)))
        args.append(sink_col)
    in_specs.append(pl.BlockSpec(memory_space=pl.ANY))
    args.append(att)
    return pl.pallas_call(
        functools.partial(_dense_attn_body, g=g, tq=tq, dk=dk, dv=dv, has_sink=sink_col is not None),
        grid=(B, kvh, n_q),
        in_specs=in_specs,
        out_specs=pl.BlockSpec((1, tq, g * dv), lambda b, h, i: (b, i + q_off, h + col_off)),
        out_shape=jax.ShapeDtypeStruct(att.shape, att.dtype),
        input_output_aliases={len(args) - 1: 0},
        compiler_params=_cp("parallel", "parallel", "parallel"),
        name="dense_attention",
    )(*args)


WIN_TQ = 512
WIN_SPAN = WIN_TQ + 2 * WINDOW


def _window_attn_body(q_ref, k_ref, v_ref, sink_ref, _, o_ref, *, g):
    q0 = pl.program_id(2) * WIN_TQ
    start = pl.multiple_of(jnp.clip(q0 - WINDOW, 0, S - WIN_SPAN), WINDOW)
    kl = k_ref[0, pl.ds(start, WIN_SPAN), :]
    vl = v_ref[0, pl.ds(start, WIN_SPAN), :]
    qi = lax.broadcasted_iota(I32, (WIN_TQ, WIN_SPAN), 0) + q0
    kj = lax.broadcasted_iota(I32, (WIN_TQ, WIN_SPAN), 1) + start
    valid = jnp.abs(qi - kj) <= WINDOW
    for j in range(g):
        q = q_ref[0, :, j * HD:(j + 1) * HD]
        s_loc = jnp.where(valid, _dot_nt(q, kl), NEG)
        s_ctx = _dot_nt(q, k_ref[0, S:T, :])
        o = _softmax_pv([(s_loc, vl), (s_ctx, v_ref[0, S:T, :])], sink_ref[0, j * WIN_TQ:(j + 1) * WIN_TQ],
                        WIN_TQ)
        o_ref[0, :, j * HD:(j + 1) * HD] = o.astype(o_ref.dtype)


def window_attention(att, q, k, v, sink_col, *, kvh, g):
    tq = WIN_TQ
    return pl.pallas_call(
        functools.partial(_window_attn_body, g=g),
        grid=(B, kvh, S // tq),
        in_specs=[
            pl.BlockSpec((1, tq, g * HD), lambda b, h, m: (b, m, h)),
            pl.BlockSpec((1, T, HD), lambda b, h, m: (b, 0, h)),
            pl.BlockSpec((1, T, HD), lambda b, h, m: (b, 0, h)),
            pl.BlockSpec((1, g * WIN_TQ, 1), lambda b, h, m: (h, 0, 0)),
            pl.BlockSpec(memory_space=pl.ANY),
        ],
        out_specs=pl.BlockSpec((1, tq, g * HD), lambda b, h, m: (b, m, h)),
        out_shape=jax.ShapeDtypeStruct(att.shape, att.dtype),
        input_output_aliases={4: 0},
        compiler_params=_cp("parallel", "parallel", "parallel"),
        name="window_attention",
    )(q, k, v, sink_col, att)


ROWS_N = S // GRID_W
NBR_G = 4
NBR_KR = NBR_G + NB_ROWS
NBR_GROUPS = ROWS_N // NBR_G
NBR_PATTERN_GROUPS = (0, 1, NBR_GROUPS - 1)


def _nbr_window_start(grp):
    return NBR_G * grp - NB_ROWS // 2


def _nbr_bias_body(rpb_ref, o_ref):
    h = pl.program_id(0)
    n_dc = 2 * NB_COLS - 1
    n_dr = 2 * NB_ROWS - 1
    qc = lax.broadcasted_iota(I32, (GRID_W, GRID_W), 0)
    kc = lax.broadcasted_iota(I32, (GRID_W, GRID_W), 1)
    dc = jnp.clip(kc - qc + NB_COLS - 1, 0, n_dc - 1)
    cs = jnp.clip(qc - NB_COLS // 2, 0, GRID_W - NB_COLS)
    col_ok = (kc >= cs) & (kc < cs + NB_COLS)
    tiles = [jnp.zeros((GRID_W, GRID_W), F32) for _ in range(n_dr)]
    for t in range(n_dc):
        hit = dc == t
        for dr in range(n_dr):
            tiles[dr] = jnp.where(hit, rpb_ref[h, dr * n_dc + t] * LOG2E, tiles[dr])
    tiles = [jnp.where(col_ok, x, NEG) for x in tiles]
    masked = jnp.full((GRID_W, GRID_W), NEG, F32)
    for p, grp in enumerate(NBR_PATTERN_GROUPS):
        start = min(max(_nbr_window_start(grp), 0), ROWS_N - NBR_KR)
        for ri in range(NBR_G):
            r = NBR_G * grp + ri
            rs = min(max(r - NB_ROWS // 2, 0), ROWS_N - NB_ROWS)
            row = [tiles[start + kr - r + NB_ROWS - 1] if rs <= start + kr < rs + NB_ROWS else masked
                   for kr in range(NBR_KR)]
            o_ref[0, p, ri * GRID_W:(ri + 1) * GRID_W, :] = jnp.concatenate(row, axis=1)


def nbr_bias_table(rpb):
    heads = rpb.shape[0]
    return pl.pallas_call(
        _nbr_bias_body,
        grid=(heads,),
        in_specs=[pl.BlockSpec(memory_space=pltpu.SMEM)],
        out_specs=pl.BlockSpec((1, 3, NBR_G * GRID_W, NBR_KR * GRID_W), lambda h: (h, 0, 0, 0)),
        out_shape=jax.ShapeDtypeStruct((heads, 3, NBR_G * GRID_W, NBR_KR * GRID_W), F32),
        compiler_params=_cp("parallel"),
        name="nbr_bias_table",
    )(rpb.astype(F32).reshape(heads, -1))


NBR_STEP = 2


def _nbr_attn_body(q_ref, k_ref, v_ref, bias_ref, _, o_ref):
    n_loc = NBR_KR * GRID_W
    rows_g = NBR_G * GRID_W
    for w in range(NBR_STEP):
        grp = pl.program_id(2) * NBR_STEP + w
        pattern = jnp.where(grp == 0, 0, jnp.where(grp == NBR_GROUPS - 1, 2, 1))
        start_row = jnp.clip(_nbr_window_start(grp), 0, ROWS_N - NBR_KR)
        start = pl.multiple_of(start_row * GRID_W, GRID_W)
        rows = slice(w * rows_g, (w + 1) * rows_g)
        q = q_ref[0, rows, :]
        s_loc = _dot_nt(q, k_ref[0, pl.ds(start, n_loc), :]) + bias_ref[0, pattern]
        s_ctx = _dot_nt(q, k_ref[0, S:T, :])
        o = _softmax_pv([(s_loc, v_ref[0, pl.ds(start, n_loc), :]), (s_ctx, v_ref[0, S:T, :])], None, rows_g)
        o_ref[0, rows, :] = o.astype(o_ref.dtype)


def neighbourhood_attention(att, q, k, v, bias_tab, *, heads, col_off):
    tq = NBR_STEP * NBR_G * GRID_W
    return pl.pallas_call(
        _nbr_attn_body,
        grid=(B, heads, NBR_GROUPS // NBR_STEP),
        in_specs=[
            pl.BlockSpec((1, tq, HD), lambda b, h, r: (b, r, h)),
            pl.BlockSpec((1, T, HD), lambda b, h, r: (b, 0, h)),
            pl.BlockSpec((1, T, HD), lambda b, h, r: (b, 0, h)),
            pl.BlockSpec((1, 3, NBR_G * GRID_W, NBR_KR * GRID_W), lambda b, h, r: (h, 0, 0, 0)),
            pl.BlockSpec(memory_space=pl.ANY),
        ],
        out_specs=pl.BlockSpec((1, tq, HD), lambda b, h, r: (b, r, h + col_off)),
        out_shape=jax.ShapeDtypeStruct(att.shape, att.dtype),
        input_output_aliases={4: 0},
        compiler_params=_cp("parallel", "parallel", "parallel"),
        name="neighbourhood_attention",
    )(q, k, v, bias_tab, att)


def _cumsum_lanes(mask, tri):
    n = mask.shape[1]
    carry = jnp.zeros((mask.shape[0], 1), F32)
    out = []
    for j in range(n // 256):
        blk = jnp.where(mask[:, j * 256:(j + 1) * 256], 1.0, 0.0).astype(BF16)
        c = _dot(blk, tri) + carry
        carry = c[:, 255:256]
        out.append(c)
    return jnp.concatenate(out, axis=1) if len(out) > 1 else out[0]


def _count(mask):
    return jnp.sum(jnp.where(mask, 1.0, 0.0), axis=1, keepdims=True)


def _topk_slots(affs, caps, tri):
    rows = affs[0].shape[0]

    def step(carry):
        bounds, _ = carry
        new, moved = [], []
        for aff, cap, (lo, hi) in zip(affs, caps, bounds):
            mid = 0.5 * (lo + hi)
            ok = _count(aff >= mid) >= cap
            nlo = jnp.where(ok, mid, lo)
            nhi = jnp.where(ok, hi, mid)
            new.append((nlo, nhi))
            moved.append(jnp.where((nlo != lo) | (nhi != hi), 1.0, 0.0))
        return tuple(new), jnp.max(functools.reduce(jnp.maximum, moved))

    start = tuple((jnp.zeros((rows, 1), F32), jnp.full((rows, 1), 2.0, F32)) for _ in affs)
    bounds, _ = lax.while_loop(lambda c: c[1] > 0.0, step, (start, jnp.float32(1.0)))
    slots = []
    for aff, cap, (lo, hi) in zip(affs, caps, bounds):
        gt = aff >= hi
        eq = (aff >= lo) & (aff < hi)
        need = cap - _count(gt)
        sel = gt | (eq & (_cumsum_lanes(eq, tri) <= need))
        pos = _cumsum_lanes(sel, tri) - 1.0
        slots.append(jnp.where(sel, pos.astype(I32), -1))
    return slots


def _router_body(x_ref, gain_ref, sc_ref, sh_ref, wr_ref, h_ref, slot_ref, aff_ref, lg_scr):
    b = pl.program_id(0)
    tt = pl.program_id(1)
    row = jnp.where(tt == LAT_TILES, CTX_ROW, b)
    y = _rms(x_ref[...], gain_ref[...])
    h = y * (1.0 + sc_ref[0, pl.ds(row, 1), :]) + sh_ref[0, pl.ds(row, 1), :]
    h_ref[...] = h.astype(BF16)
    w = wr_ref[...]
    w_hi, h_hi = w.astype(BF16), h.astype(BF16)
    w_lo = (w - w_hi.astype(F32)).astype(BF16)
    h_lo = (h - h_hi.astype(F32)).astype(BF16)
    lg_scr[tt] = _dot_nt(w_hi, h_hi) + (_dot_nt(w_hi, h_lo) + _dot_nt(w_lo, h_hi))

    @pl.when(tt == TILES - 1)
    def _():
        tri = jnp.where(lax.broadcasted_iota(I32, (256, 256), 0) <= lax.broadcasted_iota(I32, (256, 256), 1),
                        1.0, 0.0).astype(BF16)
        lat = jnp.concatenate([lg_scr[j] for j in range(LAT_TILES)], axis=1)
        affs = []
        for lg in (lat, lg_scr[LAT_TILES]):
            ex = jnp.exp(lg - lg.max(axis=0, keepdims=True))
            affs.append(ex / ex.sum(axis=0, keepdims=True))
        slots = _topk_slots(affs, (CAP_L, CAP_C), tri)
        for (off, n, _, _), aff, slot in zip(_SETS, affs, slots):
            slot_ref[0, :, off:off + n] = slot
            aff_ref[0, :, off:off + n] = aff


def router(x2d, gain, modt, l, w_router_t):
    return pl.pallas_call(
        _router_body,
        grid=(B, TILES),
        in_specs=[
            pl.BlockSpec((TM, D), lambda b, t: (b * TILES + t, 0)),
            pl.BlockSpec((1, D), lambda b, t: (0, 0)),
            pl.BlockSpec((1, 8, D), lambda b, t: (l, 0, 4)),
            pl.BlockSpec((1, 8, D), lambda b, t: (l, 0, 3)),
            pl.BlockSpec((E, D), lambda b, t: (0, 0)),
        ],
        out_specs=[
            pl.BlockSpec((TM, D), lambda b, t: (b * TILES + t, 0)),
            pl.BlockSpec((1, E, T), lambda b, t: (b, 0, 0)),
            pl.BlockSpec((1, E, T), lambda b, t: (b, 0, 0)),
        ],
        out_shape=[
            jax.ShapeDtypeStruct((R, D), BF16),
            jax.ShapeDtypeStruct((B, E, T), I32),
            jax.ShapeDtypeStruct((B, E, T), F32),
        ],
        scratch_shapes=[pltpu.VMEM((TILES, E, TM), F32)],
        compiler_params=_cp("parallel", "arbitrary"),
        name="router",
    )(x2d, gain.reshape(1, D), modt, modt, w_router_t)


_SETS = ((0, S, CAP_L, 0), (S, L, CAP_C, CAP_L))


def _dispatch_body(h_ref, slot_ref, aff_ref, xin_ref, gate_ref):
    e = pl.program_id(1)
    srow = slot_ref[0, pl.ds(e, 1), :]
    arow = aff_ref[0, pl.ds(e, 1), :]
    for off, n, cap, o0 in _SETS:
        hit = lax.broadcasted_iota(I32, (cap, n), 0) == srow[:, off:off + n]
        onehot = jnp.where(hit, 1.0, 0.0).astype(BF16)
        xin_ref[0, 0, o0:o0 + cap, :] = _dot(onehot, h_ref[0, off:off + n, :]).astype(BF16)
        gate_ref[0, 0, o0:o0 + cap, :] = jnp.sum(jnp.where(hit, arow[:, off:off + n], 0.0), axis=1, keepdims=True)


def dispatch(h2, slot, aff):
    return pl.pallas_call(
        _dispatch_body,
        grid=(B, E),
        in_specs=[
            pl.BlockSpec((1, T, D), lambda b, e: (b, 0, 0)),
            pl.BlockSpec((1, E, T), lambda b, e: (b, 0, 0)),
            pl.BlockSpec((1, E, T), lambda b, e: (b, 0, 0)),
        ],
        out_specs=[
            pl.BlockSpec((1, 1, SLOTS, D), lambda b, e: (e, b, 0, 0)),
            pl.BlockSpec((1, 1, SLOTS, 1), lambda b, e: (e, b, 0, 0)),
        ],
        out_shape=[
            jax.ShapeDtypeStruct((E, B, SLOTS, D), BF16),
            jax.ShapeDtypeStruct((E, B, SLOTS, 1), F32),
        ],
        compiler_params=_cp("parallel", "parallel"),
        name="moe_dispatch",
    )(h2.reshape(B, T, D), slot, aff)


def _ffn_body(x_ref, wg_ref, wu_ref, wd_ref, gate_ref, o_ref, acc_ref):
    f = pl.program_id(1)
    wg = wg_ref[0, 0].astype(BF16)
    wu = wu_ref[0, 0].astype(BF16)
    wd = wd_ref[0, 0].astype(BF16)

    @pl.when(f == 0)
    def _():
        acc_ref[...] = jnp.zeros_like(acc_ref)

    for r in range(x_ref.shape[1] // FFN_ROWS):
        rows = slice(r * FFN_ROWS, (r + 1) * FFN_ROWS)
        x = x_ref[0, rows, :]
        a = _dot(x, wg)
        u = _dot(x, wu)
        hid = (a * jax.nn.sigmoid(a) * u).astype(BF16)
        acc_ref[rows, :] += _dot(hid, wd)

    @pl.when(f == pl.num_programs(1) - 1)
    def _():
        o_ref[0] = (acc_ref[...] * gate_ref[0]).astype(o_ref.dtype)


def expert_ffn(xin, gates, w_gate, w_up, w_down, l):
    tf = 256
    m = B * SLOTS
    return pl.pallas_call(
        _ffn_body,
        grid=(E, FE // tf),
        in_specs=[
            pl.BlockSpec((1, m, D), lambda e, f: (e, 0, 0)),
            pl.BlockSpec((1, 1, D, tf), lambda e, f: (l, e, 0, f)),
            pl.BlockSpec((1, 1, D, tf), lambda e, f: (l, e, 0, f)),
            pl.BlockSpec((1, 1, tf, D), lambda e, f: (l, e, f, 0)),
            pl.BlockSpec((1, m, 1), lambda e, f: (e, 0, 0)),
        ],
        out_specs=pl.BlockSpec((1, m, D), lambda e, f: (e, 0, 0)),
        out_shape=jax.ShapeDtypeStruct((E, m, D), BF16),
        scratch_shapes=[pltpu.VMEM((m, D), F32)],
        compiler_params=_cp("parallel", "arbitrary"),
        name="expert_ffn",
    )(xin.reshape(E, m, D), w_gate, w_up, w_down, gates.reshape(E, m, 1))


def _combine_body(o_ref, slot_ref, x_ref, g_ref, out_ref):
    b = pl.program_id(0)
    tt = pl.program_id(2)
    tn = out_ref.shape[1]

    def run(cap, o0, row):
        acc = jnp.zeros((TM, tn), F32)
        sl = lax.broadcasted_iota(I32, (TM, cap), 1)
        for e in range(E):
            onehot = jnp.where(slot_ref[0, :, e:e + 1] == sl, 1.0, 0.0).astype(BF16)
            acc = acc + _dot(onehot, o_ref[e, 0, o0:o0 + cap, :])
        out_ref[...] = x_ref[...] + g_ref[0, pl.ds(row, 1), :] * acc

    @pl.when(tt < LAT_TILES)
    def _():
        run(CAP_L, 0, b)

    @pl.when(tt == LAT_TILES)
    def _():
        run(CAP_C, CAP_L, CTX_ROW)


def combine(o, slot_tok, x2d, modt, l):
    tn = 1024
    per = D // tn
    return pl.pallas_call(
        _combine_body,
        grid=(B, per, TILES),
        in_specs=[
            pl.BlockSpec((E, 1, SLOTS, tn), lambda b, j, t: (0, b, 0, j)),
            pl.BlockSpec((1, TM, E), lambda b, j, t: (b, t, 0)),
            pl.BlockSpec((TM, tn), lambda b, j, t: (b * TILES + t, j)),
            pl.BlockSpec((1, 8, tn), lambda b, j, t: (l, 0, 5 * per + j)),
        ],
        out_specs=pl.BlockSpec((TM, tn), lambda b, j, t: (b * TILES + t, j)),
        out_shape=jax.ShapeDtypeStruct((R, D), F32),
        compiler_params=_cp("parallel", "parallel", "parallel"),
        name="moe_combine",
    )(o.reshape(E, B, SLOTS, D), slot_tok, x2d, modt)


def moe_block(x2d, gain, modt, l, w_router_t, w_gate, w_up, w_down):
    h2, slot, aff = router(x2d, gain, modt, l, w_router_t)
    xin, gates = dispatch(h2, slot, aff)
    o = expert_ffn(xin, gates, w_gate, w_up, w_down, l)
    return combine(o, jnp.swapaxes(slot, 1, 2), x2d, modt, l)


def _rope_tables(d):
    half = d // 2
    t = jnp.arange(S)
    inv = ROPE_BASE ** (-jnp.arange(0, half, 2, dtype=F32) / half)
    parts_c, parts_s = [], []
    for pos in (t // GRID_W, t % GRID_W):
        ang = pos.astype(F32)[:, None] * inv[None, :]
        parts_c += [jnp.cos(ang), jnp.cos(ang)]
        parts_s += [-jnp.sin(ang), jnp.sin(ang)]
    cos = jnp.concatenate([jnp.concatenate(parts_c, axis=1), jnp.ones((L, d), F32)], axis=0)
    sin = jnp.concatenate([jnp.concatenate(parts_s, axis=1), jnp.zeros((L, d), F32)], axis=0)
    return cos, sin


def _rope_lanes(x, cos, sin, quarter):
    lane = lax.broadcasted_iota(I32, x.shape, 1)
    sw = jnp.where(lane % (2 * quarter) < quarter, pltpu.roll(x, HD - quarter, 1), pltpu.roll(x, quarter, 1))
    return x * cos + sw * sin


def _table_spec():
    return pl.BlockSpec((TM, HD), lambda i: (i % TILES, 0))


def _post_ab_body(p_ref, c128_ref, s128_ref, c64_ref, s64_ref, qn_ref, kvn_ref,
                  aq_ref, ak_ref, av_ref, cq_ref, ckv_ref, kr_ref):
    cos, sin = c128_ref[...], s128_ref[...]
    for h in range(8):
        aq = _rope_lanes(p_ref[:, h * HD:(h + 1) * HD], cos, sin, 32) * QK_SCALE_LOG2
        aq_ref[:, h * HD:(h + 1) * HD] = aq.astype(BF16)
    for h in range(2):
        ak_ref[:, h * HD:(h + 1) * HD] = _rope_lanes(p_ref[:, 1024 + h * HD:1024 + (h + 1) * HD], cos, sin,
                                                     32).astype(BF16)
    av_ref[...] = p_ref[:, 1280:1536].astype(BF16)
    cq_ref[...] = _rms(p_ref[:, 1536:2048], qn_ref[...]).astype(BF16)
    ckv_ref[...] = _rms(p_ref[:, 2048:2304], kvn_ref[...]).astype(BF16)
    kr = jnp.concatenate([p_ref[:, 2304:2368], jnp.zeros((TM, 64), F32)], axis=1)
    kr_ref[...] = _rope_lanes(kr, c64_ref[...], s64_ref[...], 16).astype(BF16)


def post_project_ab(proj, tabs128, tabs64pad, q_norm, kv_norm):
    widths = (1024, 256, 256, MLA_Q, MLA_KV, HD)
    return pl.pallas_call(
        _post_ab_body,
        grid=(R // TM,),
        in_specs=[pl.BlockSpec((TM, AB_IN), lambda i: (i, 0))] + [_table_spec()] * 4
        + [pl.BlockSpec((1, MLA_Q), lambda i: (0, 0)), pl.BlockSpec((1, MLA_KV), lambda i: (0, 0))],
        out_specs=[pl.BlockSpec((TM, w), lambda i: (i, 0)) for w in widths],
        out_shape=[jax.ShapeDtypeStruct((R, w), BF16) for w in widths],
        compiler_params=_cp("parallel"),
        name="post_project_ab",
    )(proj, *tabs128, *tabs64pad, q_norm.reshape(1, MLA_Q), kv_norm.reshape(1, MLA_KV))


def _q_up_body(cq_ref, w_ref, c64_ref, s64_ref, o_ref):
    y = _dot(cq_ref[...], w_ref[...])
    cos, sin = c64_ref[...], s64_ref[...]
    zeros = jnp.zeros((TM, 64), F32)
    for j in range(4):
        rot = _rope_lanes(y[:, 1024 + j * HD:1024 + (j + 1) * HD], cos, sin, 16)
        for u in range(2):
            h = 2 * j + u
            o_ref[:, 256 * h:256 * h + HD] = (y[:, h * HD:(h + 1) * HD] * MLA_SCALE_LOG2).astype(BF16)
            o_ref[:, 256 * h + HD:256 * (h + 1)] = jnp.concatenate(
                [rot[:, 64 * u:64 * (u + 1)] * MLA_SCALE_LOG2, zeros], axis=1).astype(BF16)


def mla_q_up(cq, w_uq_split, tabs64x2):
    return pl.pallas_call(
        _q_up_body,
        grid=(R // TM,),
        in_specs=[pl.BlockSpec((TM, MLA_Q), lambda i: (i, 0)), pl.BlockSpec((MLA_Q, 1536), lambda i: (0, 0)),
                  _table_spec(), _table_spec()],
        out_specs=pl.BlockSpec((TM, 2048), lambda i: (i, 0)),
        out_shape=jax.ShapeDtypeStruct((R, 2048), BF16),
        compiler_params=_cp("parallel"),
        name="mla_q_up",
    )(cq, w_uq_split, *tabs64x2)


def _kv_up_body(ckv_ref, w_ref, kr_ref, k_ref, v_ref):
    y = _dot(ckv_ref[...], w_ref[...])
    kr = kr_ref[...]
    for h in range(8):
        k_ref[:, 256 * h:256 * h + HD] = y[:, h * HD:(h + 1) * HD].astype(BF16)
        k_ref[:, 256 * h + HD:256 * (h + 1)] = kr
    v_ref[...] = y[:, 1024:].astype(BF16)


def mla_kv_up(ckv, w_ukv_split, kr):
    return pl.pallas_call(
        _kv_up_body,
        grid=(R // TM,),
        in_specs=[pl.BlockSpec((TM, MLA_KV), lambda i: (i, 0)), pl.BlockSpec((MLA_KV, 2048), lambda i: (0, 0)),
                  pl.BlockSpec((TM, HD), lambda i: (i, 0))],
        out_specs=[pl.BlockSpec((TM, 2048), lambda i: (i, 0)), pl.BlockSpec((TM, 1024), lambda i: (i, 0))],
        out_shape=[jax.ShapeDtypeStruct((R, 2048), BF16), jax.ShapeDtypeStruct((R, 1024), BF16)],
        compiler_params=_cp("parallel"),
        name="mla_kv_up",
    )(ckv, w_ukv_split, kr)


def _post_cd_body(p_ref, c128_ref, s128_ref, qn_ref, kn_ref, cq_ref, ck_ref, cv_ref, dq_ref, dk_ref, dv_ref):
    cos, sin = c128_ref[...], s128_ref[...]
    for h in range(8):
        x = _rms(p_ref[:, h * HD:(h + 1) * HD], qn_ref[...])
        cq_ref[:, h * HD:(h + 1) * HD] = (_rope_lanes(x, cos, sin, 32) * QK_SCALE_LOG2).astype(BF16)
    for h in range(2):
        x = _rms(p_ref[:, 1024 + h * HD:1024 + (h + 1) * HD], kn_ref[...])
        ck_ref[:, h * HD:(h + 1) * HD] = _rope_lanes(x, cos, sin, 32).astype(BF16)
    cv_ref[...] = p_ref[:, 1280:1536].astype(BF16)
    dq_ref[...] = (p_ref[:, 1536:2560] * QK_SCALE_LOG2).astype(BF16)
    dk_ref[...] = p_ref[:, 2560:3584].astype(BF16)
    dv_ref[...] = p_ref[:, 3584:4608].astype(BF16)


def post_project_cd(proj, tabs128, q_norm, k_norm):
    widths = (1024, 256, 256, 1024, 1024, 1024)
    return pl.pallas_call(
        _post_cd_body,
        grid=(R // TM,),
        in_specs=[pl.BlockSpec((TM, CD_IN), lambda i: (i, 0)), _table_spec(), _table_spec(),
                  pl.BlockSpec((1, HD), lambda i: (0, 0)), pl.BlockSpec((1, HD), lambda i: (0, 0))],
        out_specs=[pl.BlockSpec((TM, w), lambda i: (i, 0)) for w in widths],
        out_shape=[jax.ShapeDtypeStruct((R, w), BF16) for w in widths],
        compiler_params=_cp("parallel"),
        name="post_project_cd",
    )(proj, *tabs128, q_norm.reshape(1, HD), k_norm.reshape(1, HD))


def _sink_col(sink, kvh, g, tq):
    col = (sink.astype(F32) * LOG2E).reshape(kvh, g, 1, 1)
    return jnp.broadcast_to(col, (kvh, g, tq, 1)).reshape(kvh, g * tq, 1)


def _split_heads(w, n_first):
    k = w.shape[0]
    w3 = w.reshape(k, 8, -1)
    return jnp.concatenate([w3[:, :, :n_first].reshape(k, -1), w3[:, :, n_first:].reshape(k, -1)], axis=1)


def mixer_ab(proj, need_ctx, sink, q_norm, w_uq, kv_norm, w_ukv, tabs):
    aq, ak, av, cq, ckv, kr = post_project_ab(proj, tabs["r128"], tabs["r64pad"], q_norm, kv_norm)
    bq = mla_q_up(cq, _split_heads(w_uq, 128).astype(BF16), tabs["r64x2"])
    bk, bv = mla_kv_up(ckv, _split_heads(w_ukv, 128).astype(BF16), kr)
    aq, ak, av, bq, bk, bv = (t.reshape(B, T, -1) for t in (aq, ak, av, bq, bk, bv))

    ctx_blk = S // L
    att = jnp.zeros((B, T, D), BF16)
    att = window_attention(att, aq, ak, av, _sink_col(sink, 2, 4, WIN_TQ), kvh=2, g=4)
    att = dense_attention(att, bq, bk, bv, kvh=8, g=1, dk=256, dv=HD, tq=512, q_off=0, n_q=S // 512, nk=T,
                          k_blk=0, col_off=8)
    if need_ctx:
        att = dense_attention(att, aq, ak, av, kvh=2, g=4, dk=HD, dv=HD, tq=L, q_off=ctx_blk, n_q=1, nk=L,
                              k_blk=ctx_blk, col_off=0, sink_col=_sink_col(sink, 2, 4, L))
        att = dense_attention(att, bq, bk, bv, kvh=8, g=1, dk=256, dv=HD, tq=L, q_off=ctx_blk, n_q=1, nk=L,
                              k_blk=ctx_blk, col_off=8)
    return att.reshape(R, D)


def mixer_cd(proj, need_ctx, q_norm, k_norm, rpb, tabs):
    cq, ck, cv, dq, dk, dv = (t.reshape(B, T, -1)
                              for t in post_project_cd(proj, tabs["r128"], q_norm, k_norm))
    ctx_blk = S // L
    att = jnp.zeros((B, T, D), BF16)
    att = dense_attention(att, cq, ck, cv, kvh=2, g=4, dk=HD, dv=HD, tq=L, q_off=0, n_q=S // L, nk=T, k_blk=0,
                          col_off=0)
    att = neighbourhood_attention(att, dq, dk, dv, nbr_bias_table(rpb), heads=8, col_off=8)
    if need_ctx:
        att = dense_attention(att, cq, ck, cv, kvh=2, g=4, dk=HD, dv=HD, tq=L, q_off=ctx_blk, n_q=1, nk=L,
                              k_blk=ctx_blk, col_off=0)
        att = dense_attention(att, dq, dk, dv, kvh=8, g=1, dk=HD, dv=HD, tq=L, q_off=ctx_blk, n_q=1, nk=L,
                              k_blk=ctx_blk, col_off=8)
    return att.reshape(R, D)


def _final_norm_body(x_ref, g_ref, o_ref):
    o_ref[0] = _rms(x_ref[0], g_ref[...])


def final_rmsnorm(x3, gain):
    return pl.pallas_call(
        _final_norm_body,
        grid=(B, S // TM),
        in_specs=[pl.BlockSpec((1, TM, D), lambda b, i: (b, i, 0)), pl.BlockSpec((1, D), lambda b, i: (0, 0))],
        out_specs=pl.BlockSpec((1, TM, D), lambda b, i: (b, i, 0)),
        out_shape=jax.ShapeDtypeStruct((B, S, D), F32),
        compiler_params=_cp("parallel", "parallel"),
        name="final_norm",
    )(x3, gain.reshape(1, D))


def kernel(x, c, ctx, c_ctx, w_ada, b_ada, norm1, norm2, ab_w_in, ab_sink, ab_q_norm, ab_w_uq, ab_kv_norm,
           ab_w_ukv, ab_w_out, cd_w_in, cd_q_norm, cd_k_norm, cd_rpb, cd_w_out, w_router, w_gate, w_up, w_down,
           final_norm):
    c8 = jnp.concatenate([c, c_ctx[None, :], jnp.zeros((8 - B - 1, D), F32)], axis=0)
    modt = ada_modulation(c8, w_ada, b_ada)
    c64, s64 = _rope_tables(64)
    zeros64 = jnp.zeros((T, 64), F32)
    tabs = {
        "r128": _rope_tables(HD),
        "r64x2": (jnp.concatenate([c64, c64], axis=1), jnp.concatenate([s64, s64], axis=1)),
        "r64pad": (jnp.concatenate([c64, zeros64], axis=1), jnp.concatenate([s64, zeros64], axis=1)),
    }
    x2d = jnp.concatenate([x, ctx], axis=1).reshape(R, D)
    for l in range(DEPTH):
        need_ctx = l < DEPTH - 1
        i = l // 2
        if l % 2 == 0:
            proj = project_in(x2d, norm1[l], modt, l, ab_w_in[i].astype(BF16), AB_IN)
            att = mixer_ab(proj, need_ctx, ab_sink[i], ab_q_norm[i], ab_w_uq[i], ab_kv_norm[i], ab_w_ukv[i], tabs)
            w_out = ab_w_out[i]
        else:
            proj = project_in(x2d, norm1[l], modt, l, cd_w_in[i].astype(BF16), CD_IN // 2)
            att = mixer_cd(proj, need_ctx, cd_q_norm[i], cd_k_norm[i], cd_rpb[i], tabs)
            w_out = cd_w_out[i]
        x2d = matmul_gated_residual(att, w_out.astype(BF16), x2d, modt, l, 2)
        x2d = moe_block(x2d, norm2[l], modt, l, w_router[l].T, w_gate, w_up, w_down)
    return final_rmsnorm(x2d.reshape(B, T, D), final_norm)
```

```python
import functools

import jax
import jax.numpy as jnp
from jax import lax
from jax.experimental import pallas as pl
from jax.experimental.pallas import tpu as pltpu

F32 = jnp.float32
BF16 = jnp.bfloat16
I32 = jnp.int32

D = 2048
B = 4
S = 2048
L = 256
T = S + L
R = B * T
DEPTH = 4
GRID_W = 64
HD = 128
ROPE_BASE = 10000.0
EPS = 1e-6
NEG = -1e30
LOG2E = 1.4426950408889634
QK_SCALE_LOG2 = HD ** -0.5 * LOG2E
MLA_SCALE_LOG2 = 192 ** -0.5 * LOG2E
WINDOW = 128
BQ = 128
NB_ROWS = 8
NB_COLS = 16
E = 16
CAP_L = 2 * S // E
CAP_C = 2 * L // E
SLOTS = CAP_L + CAP_C
FE = 1024
FFN_ROWS = 384
MLA_Q = 512
MLA_KV = 256
AB_IN = 2368
CD_IN = 4608

ATTN_SUB = 256
TM = 256
TILES = T // TM
LAT_TILES = S // TM
CTX_ROW = B
VMEM_LIMIT = 56 * 1024 * 1024


def _cp(*sem):
    return pltpu.CompilerParams(dimension_semantics=sem, vmem_limit_bytes=VMEM_LIMIT)


def _dot(a, b):
    return jnp.dot(a, b, preferred_element_type=F32)


def _dot_nt(a, b):
    return lax.dot_general(a, b, (((1,), (1,)), ((), ())), preferred_element_type=F32)


def _ada_body(c_ref, w_ref, b_ref, o_ref):
    c = c_ref[...]
    a = c * jax.nn.sigmoid(c)
    o_ref[0] = _dot(a.astype(BF16), w_ref[0].astype(BF16)) + b_ref[0]


def ada_modulation(c8, w_ada, b_ada):
    tn = 1024
    n = 6 * D
    return pl.pallas_call(
        _ada_body,
        grid=(DEPTH, n // tn),
        in_specs=[
            pl.BlockSpec((8, D), lambda l, j: (0, 0)),
            pl.BlockSpec((1, D, tn), lambda l, j: (l, 0, j)),
            pl.BlockSpec((1, 1, tn), lambda l, j: (l, 0, j)),
        ],
        out_specs=pl.BlockSpec((1, 8, tn), lambda l, j: (l, 0, j)),
        out_shape=jax.ShapeDtypeStruct((DEPTH, 8, n), F32),
        compiler_params=_cp("parallel", "parallel"),
        name="ada_modulation",
    )(c8, w_ada, b_ada.reshape(DEPTH, 1, n))


def _rms(x, gain):
    return x * lax.rsqrt(jnp.mean(x * x, axis=-1, keepdims=True) + EPS) * gain


TM_IN = T // 6
TM_BIG = T // 3


def _row_mod(g_ref, tile, tm):
    per = T // tm
    r = (tile % per) * tm + lax.broadcasted_iota(I32, (tm, 1), 0)
    return jnp.where(r >= S, g_ref[0, CTX_ROW:CTX_ROW + 1, :], g_ref[0, pl.ds(tile // per, 1), :])


def _proj_in_body(*refs, n_extra, epilogue):
    x_ref, gain_ref, sc_ref, sh_ref, w_ref = refs[:5]
    tile = pl.program_id(1)
    h = _rms(x_ref[...], gain_ref[...]) * (1.0 + _row_mod(sc_ref, tile, TM_IN)) + _row_mod(sh_ref, tile, TM_IN)
    epilogue(_dot(h.astype(BF16), w_ref[...]), refs[5:5 + n_extra], refs[5 + n_extra:])


def project_in(x2d, gain, modt, l, w, col_blk, tn, n_tiles, epilogue, extras, out_widths, name):
    tabs_per_batch = T // TM_IN
    extra_specs = [pl.BlockSpec((TM_IN, HD), lambda j, i: (i % tabs_per_batch, 0)) if is_table
                   else pl.BlockSpec(a.shape, lambda j, i: (0, 0)) for a, is_table in extras]
    if n_tiles > 1:
        out_specs = [pl.BlockSpec((TM_IN, tn), lambda j, i: (i, j))]
    else:
        out_specs = [pl.BlockSpec((TM_IN, wd), lambda j, i: (i, 0)) for wd in out_widths]
    return pl.pallas_call(
        functools.partial(_proj_in_body, n_extra=len(extras), epilogue=epilogue),
        grid=(n_tiles, R // TM_IN),
        in_specs=[
            pl.BlockSpec((TM_IN, D), lambda j, i: (i, 0)),
            pl.BlockSpec((1, D), lambda j, i: (0, 0)),
            pl.BlockSpec((1, 8, D), lambda j, i: (l, 0, 1)),
            pl.BlockSpec((1, 8, D), lambda j, i: (l, 0, 0)),
            pl.BlockSpec((D, tn), lambda j, i: (0, col_blk + j)),
        ] + extra_specs,
        out_specs=out_specs,
        out_shape=[jax.ShapeDtypeStruct((R, wd), BF16) for wd in out_widths],
        compiler_params=_cp("parallel", "parallel"),
        name=name,
    )(x2d, gain.reshape(1, D), modt, modt, w, *[a for a, _ in extras])


def _mm_res_body(a_ref, b_ref, x_ref, g_ref, o_ref):
    y = _dot(a_ref[...], b_ref[...])
    o_ref[...] = x_ref[...] + _row_mod(g_ref, pl.program_id(1), TM_BIG) * y


def matmul_gated_residual(a, b, x2d, modt, l, chunk):
    tn = 1024
    k = a.shape[1]
    per = D // tn
    return pl.pallas_call(
        _mm_res_body,
        grid=(D // tn, R // TM_BIG),
        in_specs=[
            pl.BlockSpec((TM_BIG, k), lambda j, i: (i, 0)),
            pl.BlockSpec((k, tn), lambda j, i: (0, j)),
            pl.BlockSpec((TM_BIG, tn), lambda j, i: (i, j)),
            pl.BlockSpec((1, 8, tn), lambda j, i: (l, 0, chunk * per + j)),
        ],
        out_specs=pl.BlockSpec((TM_BIG, tn), lambda j, i: (i, j)),
        out_shape=jax.ShapeDtypeStruct((R, D), F32),
        compiler_params=_cp("parallel", "parallel"),
        name="matmul_gated_residual",
    )(a, b, x2d, modt)


def _softmax_pv(parts, sink, rows):
    m = parts[0][0].max(axis=-1, keepdims=True)
    for s, _ in parts[1:]:
        m = jnp.maximum(m, s.max(axis=-1, keepdims=True))
    if sink is not None:
        m = jnp.maximum(m, sink)
    den = jnp.zeros((rows, 1), F32) if sink is None else jnp.exp2(sink - m)
    acc = None
    for s, v in parts:
        p = jnp.exp2(s - m)
        den = den + p.sum(axis=-1, keepdims=True)
        pv = _dot(p.astype(BF16), v)
        acc = pv if acc is None else acc + pv
    return acc / den


def _dense_attn_body(*refs, g, tq, dk, dv, has_sink):
    if has_sink:
        q_ref, k_ref, v_ref, sink_ref, _, o_ref = refs
        sink = sink_ref[0]
    else:
        q_ref, k_ref, v_ref, _, o_ref = refs
        sink = None
    sub = min(tq, ATTN_SUB)
    for j in range(g):
        for r in range(tq // sub):
            rows = slice(r * sub, (r + 1) * sub)
            s = _dot_nt(q_ref[0, rows, j * dk:(j + 1) * dk], k_ref[0])
            sk = None if sink is None else sink[j * tq + r * sub:j * tq + (r + 1) * sub]
            o = _softmax_pv([(s, v_ref[0])], sk, sub)
            o_ref[0, rows, j * dv:(j + 1) * dv] = o.astype(o_ref.dtype)


def dense_attention(att, q, k, v, *, kvh, g, dk, dv, tq, q_off, n_q, nk, k_blk, col_off, sink_col=None,
                    k_col=0, v_col=0):
    in_specs = [
        pl.BlockSpec((1, tq, g * dk), lambda b, h, i: (b, i + q_off, h)),
        pl.BlockSpec((1, nk, dk), lambda b, h, i: (b, k_blk, h + k_col)),
        pl.BlockSpec((1, nk, dv), lambda b, h, i: (b, k_blk, h + v_col)),
    ]
    args = [q, k, v]
    if sink_col is not None:
        in_specs.append(pl.BlockSpec((1, g * tq, 1), lambda b, h, i: (h, 0, 0)))
        args.append(sink_col)
    in_specs.append(pl.BlockSpec(memory_space=pl.ANY))
    args.append(att)
    return pl.pallas_call(
        functools.partial(_dense_attn_body, g=g, tq=tq, dk=dk, dv=dv, has_sink=sink_col is not None),
        grid=(B, kvh, n_q),
        in_specs=in_specs,
        out_specs=pl.BlockSpec((1, tq, g * dv), lambda b, h, i: (b, i + q_off, h + col_off)),
        out_shape=jax.ShapeDtypeStruct(att.shape, att.dtype),
        input_output_aliases={len(args) - 1: 0},
        compiler_params=_cp("parallel", "parallel", "parallel"),
        name="dense_attention",
    )(*args)


WIN_TQ = 512
WIN_SPAN = WIN_TQ + 2 * WINDOW


def _window_attn_body(q_ref, k_ref, v_ref, sink_ref, _, o_ref, *, g):
    q0 = pl.program_id(2) * WIN_TQ
    start = pl.multiple_of(jnp.clip(q0 - WINDOW, 0, S - WIN_SPAN), WINDOW)
    kl = k_ref[0, pl.ds(start, WIN_SPAN), :]
    vl = v_ref[0, pl.ds(start, WIN_SPAN), :]
    qi = lax.broadcasted_iota(I32, (WIN_TQ, WIN_SPAN), 0) + q0
    kj = lax.broadcasted_iota(I32, (WIN_TQ, WIN_SPAN), 1) + start
    valid = jnp.abs(qi - kj) <= WINDOW
    for j in range(g):
        q = q_ref[0, :, j * HD:(j + 1) * HD]
        s_loc = jnp.where(valid, _dot_nt(q, kl), NEG)
        s_ctx = _dot_nt(q, k_ref[0, S:T, :])
        o = _softmax_pv([(s_loc, vl), (s_ctx, v_ref[0, S:T, :])], sink_ref[0, j * WIN_TQ:(j + 1) * WIN_TQ],
                        WIN_TQ)
        o_ref[0, :, j * HD:(j + 1) * HD] = o.astype(o_ref.dtype)


def window_attention(att, q, k, v, sink_col, *, kvh, g):
    tq = WIN_TQ
    return pl.pallas_call(
        functools.partial(_window_attn_body, g=g),
        grid=(B, kvh, S // tq),
        in_specs=[
            pl.BlockSpec((1, tq, g * HD), lambda b, h, m: (b, m, h)),
            pl.BlockSpec((1, T, HD), lambda b, h, m: (b, 0, h)),
            pl.BlockSpec((1, T, HD), lambda b, h, m: (b, 0, h)),
            pl.BlockSpec((1, g * WIN_TQ, 1), lambda b, h, m: (h, 0, 0)),
            pl.BlockSpec(memory_space=pl.ANY),
        ],
        out_specs=pl.BlockSpec((1, tq, g * HD), lambda b, h, m: (b, m, h)),
        out_shape=jax.ShapeDtypeStruct(att.shape, att.dtype),
        input_output_aliases={4: 0},
        compiler_params=_cp("parallel", "parallel", "parallel"),
        name="window_attention",
    )(q, k, v, sink_col, att)


ROWS_N = S // GRID_W
NBR_G = 4
NBR_KR = NBR_G + NB_ROWS
NBR_GROUPS = ROWS_N // NBR_G
NBR_PATTERN_GROUPS = (0, 1, NBR_GROUPS - 1)


def _nbr_window_start(grp):
    return NBR_G * grp - NB_ROWS // 2


def _nbr_bias_body(rpb_ref, o_ref):
    h = pl.program_id(0)
    n_dc = 2 * NB_COLS - 1
    n_dr = 2 * NB_ROWS - 1
    qc = lax.broadcasted_iota(I32, (GRID_W, GRID_W), 0)
    kc = lax.broadcasted_iota(I32, (GRID_W, GRID_W), 1)
    dc = jnp.clip(kc - qc + NB_COLS - 1, 0, n_dc - 1)
    cs = jnp.clip(qc - NB_COLS // 2, 0, GRID_W - NB_COLS)
    col_ok = (kc >= cs) & (kc < cs + NB_COLS)
    tiles = [jnp.zeros((GRID_W, GRID_W), F32) for _ in range(n_dr)]
    for t in range(n_dc):
        hit = dc == t
        for dr in range(n_dr):
            tiles[dr] = jnp.where(hit, rpb_ref[h, dr * n_dc + t] * LOG2E, tiles[dr])
    tiles = [jnp.where(col_ok, x, NEG) for x in tiles]
    masked = jnp.full((GRID_W, GRID_W), NEG, F32)
    for p, grp in enumerate(NBR_PATTERN_GROUPS):
        start = min(max(_nbr_window_start(grp), 0), ROWS_N - NBR_KR)
        for ri in range(NBR_G):
            r = NBR_G * grp + ri
            rs = min(max(r - NB_ROWS // 2, 0), ROWS_N - NB_ROWS)
            row = [tiles[start + kr - r + NB_ROWS - 1] if rs <= start + kr < rs + NB_ROWS else masked
                   for kr in range(NBR_KR)]
            o_ref[0, p, ri * GRID_W:(ri + 1) * GRID_W, :] = jnp.concatenate(row, axis=1)


def nbr_bias_table(rpb):
    heads = rpb.shape[0]
    return pl.pallas_call(
        _nbr_bias_body,
        grid=(heads,),
        in_specs=[pl.BlockSpec(memory_space=pltpu.SMEM)],
        out_specs=pl.BlockSpec((1, 3, NBR_G * GRID_W, NBR_KR * GRID_W), lambda h: (h, 0, 0, 0)),
        out_shape=jax.ShapeDtypeStruct((heads, 3, NBR_G * GRID_W, NBR_KR * GRID_W), F32),
        compiler_params=_cp("parallel"),
        name="nbr_bias_table",
    )(rpb.astype(F32).reshape(heads, -1))


NBR_STEP = 2


def _nbr_attn_body(q_ref, k_ref, v_ref, bias_ref, _, o_ref):
    n_loc = NBR_KR * GRID_W
    rows_g = NBR_G * GRID_W
    for w in range(NBR_STEP):
        grp = pl.program_id(2) * NBR_STEP + w
        pattern = jnp.where(grp == 0, 0, jnp.where(grp == NBR_GROUPS - 1, 2, 1))
        start_row = jnp.clip(_nbr_window_start(grp), 0, ROWS_N - NBR_KR)
        start = pl.multiple_of(start_row * GRID_W, GRID_W)
        rows = slice(w * rows_g, (w + 1) * rows_g)
        q = q_ref[0, rows, :]
        s_loc = _dot_nt(q, k_ref[0, pl.ds(start, n_loc), :]) + bias_ref[0, pattern]
        s_ctx = _dot_nt(q, k_ref[0, S:T, :])
        o = _softmax_pv([(s_loc, v_ref[0, pl.ds(start, n_loc), :]), (s_ctx, v_ref[0, S:T, :])], None, rows_g)
        o_ref[0, rows, :] = o.astype(o_ref.dtype)


def neighbourhood_attention(att, q, k, v, bias_tab, *, heads, col_off, k_col=0, v_col=0):
    tq = NBR_STEP * NBR_G * GRID_W
    return pl.pallas_call(
        _nbr_attn_body,
        grid=(B, heads, NBR_GROUPS // NBR_STEP),
        in_specs=[
            pl.BlockSpec((1, tq, HD), lambda b, h, r: (b, r, h)),
            pl.BlockSpec((1, T, HD), lambda b, h, r: (b, 0, h + k_col)),
            pl.BlockSpec((1, T, HD), lambda b, h, r: (b, 0, h + v_col)),
            pl.BlockSpec((1, 3, NBR_G * GRID_W, NBR_KR * GRID_W), lambda b, h, r: (h, 0, 0, 0)),
            pl.BlockSpec(memory_space=pl.ANY),
        ],
        out_specs=pl.BlockSpec((1, tq, HD), lambda b, h, r: (b, r, h + col_off)),
        out_shape=jax.ShapeDtypeStruct(att.shape, att.dtype),
        input_output_aliases={4: 0},
        compiler_params=_cp("parallel", "parallel", "parallel"),
        name="neighbourhood_attention",
    )(q, k, v, bias_tab, att)


def _cumsum_lanes(mask, tri):
    n = mask.shape[1]
    carry = jnp.zeros((mask.shape[0], 1), F32)
    out = []
    for j in range(n // 256):
        blk = jnp.where(mask[:, j * 256:(j + 1) * 256], 1.0, 0.0).astype(BF16)
        c = _dot(blk, tri) + carry
        carry = c[:, 255:256]
        out.append(c)
    return jnp.concatenate(out, axis=1) if len(out) > 1 else out[0]


def _count(mask):
    return jnp.sum(jnp.where(mask, 1.0, 0.0), axis=1, keepdims=True)


def _topk_slots(affs, caps, tri):
    rows = affs[0].shape[0]

    def step(carry):
        bounds, _ = carry
        new, moved = [], []
        for aff, cap, (lo, hi) in zip(affs, caps, bounds):
            mid = 0.5 * (lo + hi)
            ok = _count(aff >= mid) >= cap
            nlo = jnp.where(ok, mid, lo)
            nhi = jnp.where(ok, hi, mid)
            new.append((nlo, nhi))
            moved.append(jnp.where((nlo != lo) | (nhi != hi), 1.0, 0.0))
        return tuple(new), jnp.max(functools.reduce(jnp.maximum, moved))

    start = tuple((jnp.zeros((rows, 1), F32), jnp.full((rows, 1), 2.0, F32)) for _ in affs)
    bounds, _ = lax.while_loop(lambda c: c[1] > 0.0, step, (start, jnp.float32(1.0)))
    slots = []
    for aff, cap, (lo, hi) in zip(affs, caps, bounds):
        gt = aff >= hi
        eq = (aff >= lo) & (aff < hi)
        need = cap - _count(gt)
        sel = gt | (eq & (_cumsum_lanes(eq, tri) <= need))
        pos = _cumsum_lanes(sel, tri) - 1.0
        slots.append(jnp.where(sel, pos.astype(I32), -1))
    return slots


def _router_body(x_ref, gain_ref, sc_ref, sh_ref, wr_ref, h_ref, slot_ref, aff_ref, lg_scr):
    b = pl.program_id(0)
    tt = pl.program_id(1)
    row = jnp.where(tt == LAT_TILES, CTX_ROW, b)
    y = _rms(x_ref[...], gain_ref[...])
    h = y * (1.0 + sc_ref[0, pl.ds(row, 1), :]) + sh_ref[0, pl.ds(row, 1), :]
    h_ref[...] = h.astype(BF16)
    w = wr_ref[...]
    w_hi, h_hi = w.astype(BF16), h.astype(BF16)
    w_lo = (w - w_hi.astype(F32)).astype(BF16)
    h_lo = (h - h_hi.astype(F32)).astype(BF16)
    lg_scr[tt] = _dot_nt(w_hi, h_hi) + (_dot_nt(w_hi, h_lo) + _dot_nt(w_lo, h_hi))

    @pl.when(tt == TILES - 1)
    def _():
        tri = jnp.where(lax.broadcasted_iota(I32, (256, 256), 0) <= lax.broadcasted_iota(I32, (256, 256), 1),
                        1.0, 0.0).astype(BF16)
        lat = jnp.concatenate([lg_scr[j] for j in range(LAT_TILES)], axis=1)
        affs = []
        for lg in (lat, lg_scr[LAT_TILES]):
            ex = jnp.exp(lg - lg.max(axis=0, keepdims=True))
            affs.append(ex / ex.sum(axis=0, keepdims=True))
        slots = _topk_slots(affs, (CAP_L, CAP_C), tri)
        for (off, n, _, _), aff, slot in zip(_SETS, affs, slots):
            slot_ref[0, :, off:off + n] = slot
            aff_ref[0, :, off:off + n] = aff


def router(x2d, gain, modt, l, w_router_t):
    return pl.pallas_call(
        _router_body,
        grid=(B, TILES),
        in_specs=[
            pl.BlockSpec((TM, D), lambda b, t: (b * TILES + t, 0)),
            pl.BlockSpec((1, D), lambda b, t: (0, 0)),
            pl.BlockSpec((1, 8, D), lambda b, t: (l, 0, 4)),
            pl.BlockSpec((1, 8, D), lambda b, t: (l, 0, 3)),
            pl.BlockSpec((E, D), lambda b, t: (0, 0)),
        ],
        out_specs=[
            pl.BlockSpec((TM, D), lambda b, t: (b * TILES + t, 0)),
            pl.BlockSpec((1, E, T), lambda b, t: (b, 0, 0)),
            pl.BlockSpec((1, E, T), lambda b, t: (b, 0, 0)),
        ],
        out_shape=[
            jax.ShapeDtypeStruct((R, D), BF16),
            jax.ShapeDtypeStruct((B, E, T), I32),
            jax.ShapeDtypeStruct((B, E, T), F32),
        ],
        scratch_shapes=[pltpu.VMEM((TILES, E, TM), F32)],
        compiler_params=_cp("parallel", "arbitrary"),
        name="router",
    )(x2d, gain.reshape(1, D), modt, modt, w_router_t)


_SETS = ((0, S, CAP_L, 0), (S, L, CAP_C, CAP_L))


def _dispatch_body(h_ref, slot_ref, aff_ref, xin_ref, gate_ref):
    e = pl.program_id(1)
    srow = slot_ref[0, pl.ds(e, 1), :]
    arow = aff_ref[0, pl.ds(e, 1), :]
    for off, n, cap, o0 in _SETS:
        hit = lax.broadcasted_iota(I32, (cap, n), 0) == srow[:, off:off + n]
        onehot = jnp.where(hit, 1.0, 0.0).astype(BF16)
        xin_ref[0, 0, o0:o0 + cap, :] = _dot(onehot, h_ref[0, off:off + n, :]).astype(BF16)
        gate_ref[0, 0, o0:o0 + cap, :] = jnp.sum(jnp.where(hit, arow[:, off:off + n], 0.0), axis=1, keepdims=True)


def dispatch(h2, slot, aff):
    return pl.pallas_call(
        _dispatch_body,
        grid=(B, E),
        in_specs=[
            pl.BlockSpec((1, T, D), lambda b, e: (b, 0, 0)),
            pl.BlockSpec((1, E, T), lambda b, e: (b, 0, 0)),
            pl.BlockSpec((1, E, T), lambda b, e: (b, 0, 0)),
        ],
        out_specs=[
            pl.BlockSpec((1, 1, SLOTS, D), lambda b, e: (e, b, 0, 0)),
            pl.BlockSpec((1, 1, SLOTS, 1), lambda b, e: (e, b, 0, 0)),
        ],
        out_shape=[
            jax.ShapeDtypeStruct((E, B, SLOTS, D), BF16),
            jax.ShapeDtypeStruct((E, B, SLOTS, 1), F32),
        ],
        compiler_params=_cp("parallel", "parallel"),
        name="moe_dispatch",
    )(h2.reshape(B, T, D), slot, aff)


def _ffn_body(x_ref, wg_ref, wu_ref, wd_ref, gate_ref, o_ref, acc_ref):
    f = pl.program_id(1)
    wg = wg_ref[0, 0].astype(BF16)
    wu = wu_ref[0, 0].astype(BF16)
    wd = wd_ref[0, 0].astype(BF16)

    @pl.when(f == 0)
    def _():
        acc_ref[...] = jnp.zeros_like(acc_ref)

    for r in range(x_ref.shape[1] // FFN_ROWS):
        rows = slice(r * FFN_ROWS, (r + 1) * FFN_ROWS)
        x = x_ref[0, rows, :]
        a = _dot(x, wg)
        u = _dot(x, wu)
        hid = (a * jax.nn.sigmoid(a) * u).astype(BF16)
        acc_ref[rows, :] += _dot(hid, wd)

    @pl.when(f == pl.num_programs(1) - 1)
    def _():
        o_ref[0] = (acc_ref[...] * gate_ref[0]).astype(o_ref.dtype)


def expert_ffn(xin, gates, w_gate, w_up, w_down, l):
    tf = 256
    m = B * SLOTS
    return pl.pallas_call(
        _ffn_body,
        grid=(E, FE // tf),
        in_specs=[
            pl.BlockSpec((1, m, D), lambda e, f: (e, 0, 0)),
            pl.BlockSpec((1, 1, D, tf), lambda e, f: (l, e, 0, f)),
            pl.BlockSpec((1, 1, D, tf), lambda e, f: (l, e, 0, f)),
            pl.BlockSpec((1, 1, tf, D), lambda e, f: (l, e, f, 0)),
            pl.BlockSpec((1, m, 1), lambda e, f: (e, 0, 0)),
        ],
        out_specs=pl.BlockSpec((1, m, D), lambda e, f: (e, 0, 0)),
        out_shape=jax.ShapeDtypeStruct((E, m, D), BF16),
        scratch_shapes=[pltpu.VMEM((m, D), F32)],
        compiler_params=_cp("parallel", "arbitrary"),
        name="expert_ffn",
    )(xin.reshape(E, m, D), w_gate, w_up, w_down, gates.reshape(E, m, 1))


def _combine_body(o_ref, slot_ref, x_ref, g_ref, out_ref):
    b = pl.program_id(0)
    tt = pl.program_id(2)
    tn = out_ref.shape[1]

    def run(cap, o0, row):
        acc = jnp.zeros((TM, tn), F32)
        sl = lax.broadcasted_iota(I32, (TM, cap), 1)
        for e in range(E):
            onehot = jnp.where(slot_ref[0, :, e:e + 1] == sl, 1.0, 0.0).astype(BF16)
            acc = acc + _dot(onehot, o_ref[e, 0, o0:o0 + cap, :])
        out_ref[...] = x_ref[...] + g_ref[0, pl.ds(row, 1), :] * acc

    @pl.when(tt < LAT_TILES)
    def _():
        run(CAP_L, 0, b)

    @pl.when(tt == LAT_TILES)
    def _():
        run(CAP_C, CAP_L, CTX_ROW)


def combine(o, slot_tok, x2d, modt, l):
    tn = 1024
    per = D // tn
    return pl.pallas_call(
        _combine_body,
        grid=(B, per, TILES),
        in_specs=[
            pl.BlockSpec((E, 1, SLOTS, tn), lambda b, j, t: (0, b, 0, j)),
            pl.BlockSpec((1, TM, E), lambda b, j, t: (b, t, 0)),
            pl.BlockSpec((TM, tn), lambda b, j, t: (b * TILES + t, j)),
            pl.BlockSpec((1, 8, tn), lambda b, j, t: (l, 0, 5 * per + j)),
        ],
        out_specs=pl.BlockSpec((TM, tn), lambda b, j, t: (b * TILES + t, j)),
        out_shape=jax.ShapeDtypeStruct((R, D), F32),
        compiler_params=_cp("parallel", "parallel", "parallel"),
        name="moe_combine",
    )(o.reshape(E, B, SLOTS, D), slot_tok, x2d, modt)


def moe_block(x2d, gain, modt, l, w_router_t, w_gate, w_up, w_down):
    h2, slot, aff = router(x2d, gain, modt, l, w_router_t)
    xin, gates = dispatch(h2, slot, aff)
    o = expert_ffn(xin, gates, w_gate, w_up, w_down, l)
    return combine(o, jnp.swapaxes(slot, 1, 2), x2d, modt, l)


def _rope_tables(d):
    half = d // 2
    t = jnp.arange(S)
    inv = ROPE_BASE ** (-jnp.arange(0, half, 2, dtype=F32) / half)
    parts_c, parts_s = [], []
    for pos in (t // GRID_W, t % GRID_W):
        ang = pos.astype(F32)[:, None] * inv[None, :]
        parts_c += [jnp.cos(ang), jnp.cos(ang)]
        parts_s += [-jnp.sin(ang), jnp.sin(ang)]
    cos = jnp.concatenate([jnp.concatenate(parts_c, axis=1), jnp.ones((L, d), F32)], axis=0)
    sin = jnp.concatenate([jnp.concatenate(parts_s, axis=1), jnp.zeros((L, d), F32)], axis=0)
    return cos, sin


def _rope_lanes(x, cos, sin, quarter):
    lane = lax.broadcasted_iota(I32, x.shape, 1)
    sw = jnp.where(lane % (2 * quarter) < quarter, pltpu.roll(x, HD - quarter, 1), pltpu.roll(x, quarter, 1))
    return x * cos + sw * sin


def _table_spec():
    return pl.BlockSpec((TM, HD), lambda i: (i % TILES, 0))


PROJ_A = 1536


def _ab_attn_epilogue(y, extras, outs):
    (c_ref, s_ref), (aq_ref, ak_ref, av_ref) = extras, outs
    cos, sin = c_ref[...], s_ref[...]
    for h in range(8):
        aq = _rope_lanes(y[:, h * HD:(h + 1) * HD], cos, sin, 32) * QK_SCALE_LOG2
        aq_ref[:, h * HD:(h + 1) * HD] = aq.astype(BF16)
    for h in range(2):
        ak_ref[:, h * HD:(h + 1) * HD] = _rope_lanes(y[:, 1024 + h * HD:1024 + (h + 1) * HD], cos, sin,
                                                     32).astype(BF16)
    av_ref[...] = y[:, 1280:PROJ_A].astype(BF16)


def _ab_mla_epilogue(y, extras, outs):
    (c64_ref, s64_ref, qn_ref, kvn_ref), (cq_ref, ckv_ref, kr_ref) = extras, outs
    cq_ref[...] = _rms(y[:, :MLA_Q], qn_ref[...]).astype(BF16)
    ckv_ref[...] = _rms(y[:, MLA_Q:MLA_Q + MLA_KV], kvn_ref[...]).astype(BF16)
    kr = jnp.concatenate([y[:, MLA_Q + MLA_KV:], jnp.zeros((TM_IN, 64), F32)], axis=1)
    kr_ref[...] = _rope_lanes(kr, c64_ref[...], s64_ref[...], 16).astype(BF16)


def _q_up_body(cq_ref, w_ref, c64_ref, s64_ref, o_ref):
    y = _dot(cq_ref[...], w_ref[...])
    cos, sin = c64_ref[...], s64_ref[...]
    zeros = jnp.zeros((TM, 64), F32)
    for j in range(4):
        rot = _rope_lanes(y[:, 1024 + j * HD:1024 + (j + 1) * HD], cos, sin, 16)
        for u in range(2):
            h = 2 * j + u
            o_ref[:, 256 * h:256 * h + HD] = (y[:, h * HD:(h + 1) * HD] * MLA_SCALE_LOG2).astype(BF16)
            o_ref[:, 256 * h + HD:256 * (h + 1)] = jnp.concatenate(
                [rot[:, 64 * u:64 * (u + 1)] * MLA_SCALE_LOG2, zeros], axis=1).astype(BF16)


def mla_q_up(cq, w_uq_split, tabs64x2):
    return pl.pallas_call(
        _q_up_body,
        grid=(R // TM,),
        in_specs=[pl.BlockSpec((TM, MLA_Q), lambda i: (i, 0)), pl.BlockSpec((MLA_Q, 1536), lambda i: (0, 0)),
                  _table_spec(), _table_spec()],
        out_specs=pl.BlockSpec((TM, 2048), lambda i: (i, 0)),
        out_shape=jax.ShapeDtypeStruct((R, 2048), BF16),
        compiler_params=_cp("parallel"),
        name="mla_q_up",
    )(cq, w_uq_split, *tabs64x2)


def _kv_up_body(ckv_ref, w_ref, kr_ref, k_ref, v_ref):
    y = _dot(ckv_ref[...], w_ref[...])
    kr = kr_ref[...]
    for h in range(8):
        k_ref[:, 256 * h:256 * h + HD] = y[:, h * HD:(h + 1) * HD].astype(BF16)
        k_ref[:, 256 * h + HD:256 * (h + 1)] = kr
    v_ref[...] = y[:, 1024:].astype(BF16)


def mla_kv_up(ckv, w_ukv_split, kr):
    return pl.pallas_call(
        _kv_up_body,
        grid=(R // TM,),
        in_specs=[pl.BlockSpec((TM, MLA_KV), lambda i: (i, 0)), pl.BlockSpec((MLA_KV, 2048), lambda i: (0, 0)),
                  pl.BlockSpec((TM, HD), lambda i: (i, 0))],
        out_specs=[pl.BlockSpec((TM, 2048), lambda i: (i, 0)), pl.BlockSpec((TM, 1024), lambda i: (i, 0))],
        out_shape=[jax.ShapeDtypeStruct((R, 2048), BF16), jax.ShapeDtypeStruct((R, 1024), BF16)],
        compiler_params=_cp("parallel"),
        name="mla_kv_up",
    )(ckv, w_ukv_split, kr)


def _cd_attn_epilogue(y, extras, outs):
    (c_ref, s_ref, qn_ref, kn_ref), (cq_ref, ck_ref, cv_ref) = extras, outs
    cos, sin = c_ref[...], s_ref[...]
    for h in range(8):
        x = _rms(y[:, h * HD:(h + 1) * HD], qn_ref[...])
        cq_ref[:, h * HD:(h + 1) * HD] = (_rope_lanes(x, cos, sin, 32) * QK_SCALE_LOG2).astype(BF16)
    for h in range(2):
        x = _rms(y[:, 1024 + h * HD:1024 + (h + 1) * HD], kn_ref[...])
        ck_ref[:, h * HD:(h + 1) * HD] = _rope_lanes(x, cos, sin, 32).astype(BF16)
    cv_ref[...] = y[:, 1280:PROJ_A].astype(BF16)


def _cd_nbr_epilogue(y, extras, outs):
    j = pl.program_id(0)
    outs[0][...] = (y * extras[0][pl.ds(j, 1), :]).astype(BF16)


def _sink_col(sink, kvh, g, tq):
    col = (sink.astype(F32) * LOG2E).reshape(kvh, g, 1, 1)
    return jnp.broadcast_to(col, (kvh, g, tq, 1)).reshape(kvh, g * tq, 1)


def _split_heads(w, n_first):
    k = w.shape[0]
    w3 = w.reshape(k, 8, -1)
    return jnp.concatenate([w3[:, :, :n_first].reshape(k, -1), w3[:, :, n_first:].reshape(k, -1)], axis=1)


def mixer_ab(x2d, gain, modt, l, need_ctx, w_in, sink, q_norm, w_uq, kv_norm, w_ukv, tabs):
    w_in = w_in.astype(BF16)
    table = [(t, True) for t in tabs["r128"]]
    aq, ak, av = project_in(x2d, gain, modt, l, w_in, 0, PROJ_A, 1, _ab_attn_epilogue, table,
                            (1024, 256, 256), "project_in_ab_attn")
    extras = [(t, True) for t in tabs["r64pad"]] + [(q_norm.reshape(1, MLA_Q), False),
                                                    (kv_norm.reshape(1, MLA_KV), False)]
    cq, ckv, kr = project_in(x2d, gain, modt, l, w_in[:, PROJ_A:], 0, AB_IN - PROJ_A, 1, _ab_mla_epilogue, extras,
                             (MLA_Q, MLA_KV, HD), "project_in_ab_mla")
    bq = mla_q_up(cq, _split_heads(w_uq, 128).astype(BF16), tabs["r64x2"])
    bk, bv = mla_kv_up(ckv, _split_heads(w_ukv, 128).astype(BF16), kr)
    aq, ak, av, bq, bk, bv = (t.reshape(B, T, -1) for t in (aq, ak, av, bq, bk, bv))

    ctx_blk = S // L
    att = jnp.zeros((B, T, D), BF16)
    att = window_attention(att, aq, ak, av, _sink_col(sink, 2, 4, WIN_TQ), kvh=2, g=4)
    att = dense_attention(att, bq, bk, bv, kvh=8, g=1, dk=256, dv=HD, tq=512, q_off=0, n_q=S // 512, nk=T,
                          k_blk=0, col_off=8)
    if need_ctx:
        att = dense_attention(att, aq, ak, av, kvh=2, g=4, dk=HD, dv=HD, tq=L, q_off=ctx_blk, n_q=1, nk=L,
                              k_blk=ctx_blk, col_off=0, sink_col=_sink_col(sink, 2, 4, L))
        att = dense_attention(att, bq, bk, bv, kvh=8, g=1, dk=256, dv=HD, tq=L, q_off=ctx_blk, n_q=1, nk=L,
                              k_blk=ctx_blk, col_off=8)
    return att.reshape(R, D)


def mixer_cd(x2d, gain, modt, l, need_ctx, w_in, q_norm, k_norm, rpb, tabs):
    w_in = w_in.astype(BF16)
    extras = [(t, True) for t in tabs["r128"]] + [(q_norm.reshape(1, HD), False), (k_norm.reshape(1, HD), False)]
    cq, ck, cv = (t.reshape(B, T, -1) for t in project_in(
        x2d, gain, modt, l, w_in, 0, PROJ_A, 1, _cd_attn_epilogue, extras, (1024, 256, 256), "project_in_cd_attn"))
    n_d = CD_IN - PROJ_A
    q_scale = jnp.concatenate([jnp.full((1024,), QK_SCALE_LOG2, F32), jnp.ones((n_d - 1024,), F32)])
    dqkv, = project_in(x2d, gain, modt, l, w_in, 1, PROJ_A, n_d // PROJ_A, _cd_nbr_epilogue,
                       [(q_scale.reshape(n_d // PROJ_A, PROJ_A), False)], (n_d,), "project_in_cd_nbr")
    dqkv = dqkv.reshape(B, T, n_d)
    ctx_blk = S // L
    att = jnp.zeros((B, T, D), BF16)
    att = dense_attention(att, cq, ck, cv, kvh=2, g=4, dk=HD, dv=HD, tq=L, q_off=0, n_q=S // L, nk=T, k_blk=0,
                          col_off=0)
    att = neighbourhood_attention(att, dqkv, dqkv, dqkv, nbr_bias_table(rpb), heads=8, col_off=8, k_col=8,
                                  v_col=16)
    if need_ctx:
        att = dense_attention(att, cq, ck, cv, kvh=2, g=4, dk=HD, dv=HD, tq=L, q_off=ctx_blk, n_q=1, nk=L,
                              k_blk=ctx_blk, col_off=0)
        att = dense_attention(att, dqkv, dqkv, dqkv, kvh=8, g=1, dk=HD, dv=HD, tq=L, q_off=ctx_blk, n_q=1, nk=L,
                              k_blk=ctx_blk, col_off=8, k_col=8, v_col=16)
    return att.reshape(R, D)


def _final_norm_body(x_ref, g_ref, o_ref):
    o_ref[0] = _rms(x_ref[0], g_ref[...])


def final_rmsnorm(x3, gain):
    return pl.pallas_call(
        _final_norm_body,
        grid=(B, S // TM),
        in_specs=[pl.BlockSpec((1, TM, D), lambda b, i: (b, i, 0)), pl.BlockSpec((1, D), lambda b, i: (0, 0))],
        out_specs=pl.BlockSpec((1, TM, D), lambda b, i: (b, i, 0)),
        out_shape=jax.ShapeDtypeStruct((B, S, D), F32),
        compiler_params=_cp("parallel", "parallel"),
        name="final_norm",
    )(x3, gain.reshape(1, D))


def kernel(x, c, ctx, c_ctx, w_ada, b_ada, norm1, norm2, ab_w_in, ab_sink, ab_q_norm, ab_w_uq, ab_kv_norm,
           ab_w_ukv, ab_w_out, cd_w_in, cd_q_norm, cd_k_norm, cd_rpb, cd_w_out, w_router, w_gate, w_up, w_down,
           final_norm):
    c8 = jnp.concatenate([c, c_ctx[None, :], jnp.zeros((8 - B - 1, D), F32)], axis=0)
    modt = ada_modulation(c8, w_ada, b_ada)
    c64, s64 = _rope_tables(64)
    zeros64 = jnp.zeros((T, 64), F32)
    tabs = {
        "r128": _rope_tables(HD),
        "r64x2": (jnp.concatenate([c64, c64], axis=1), jnp.concatenate([s64, s64], axis=1)),
        "r64pad": (jnp.concatenate([c64, zeros64], axis=1), jnp.concatenate([s64, zeros64], axis=1)),
    }
    x2d = jnp.concatenate([x, ctx], axis=1).reshape(R, D)
    for l in range(DEPTH):
        need_ctx = l < DEPTH - 1
        i = l // 2
        if l % 2 == 0:
            att = mixer_ab(x2d, norm1[l], modt, l, need_ctx, ab_w_in[i], ab_sink[i], ab_q_norm[i], ab_w_uq[i],
                           ab_kv_norm[i], ab_w_ukv[i], tabs)
            w_out = ab_w_out[i]
        else:
            att = mixer_cd(x2d, norm1[l], modt, l, need_ctx, cd_w_in[i], cd_q_norm[i], cd_k_norm[i], cd_rpb[i],
                           tabs)
            w_out = cd_w_out[i]
        x2d = matmul_gated_residual(att, w_out.astype(BF16), x2d, modt, l, 2)
        x2d = moe_block(x2d, norm2[l], modt, l, w_router[l].T, w_gate, w_up, w_down)
    return final_rmsnorm(x2d.reshape(B, T, D), final_norm)
```

```python
import functools

import jax
import jax.numpy as jnp
from jax import lax
from jax.experimental import pallas as pl
from jax.experimental.pallas import tpu as pltpu

F32 = jnp.float32
BF16 = jnp.bfloat16
I32 = jnp.int32

D = 2048
B = 4
S = 2048
L = 256
T = S + L
R = B * T
DEPTH = 4
GRID_W = 64
HD = 128
ROPE_BASE = 10000.0
EPS = 1e-6
NEG = -1e30
LOG2E = 1.4426950408889634
QK_SCALE_LOG2 = HD ** -0.5 * LOG2E
MLA_SCALE_LOG2 = 192 ** -0.5 * LOG2E
WINDOW = 128
BQ = 128
NB_ROWS = 8
NB_COLS = 16
E = 16
CAP_L = 2 * S // E
CAP_C = 2 * L // E
SLOTS = CAP_L + CAP_C
FE = 1024
FFN_ROWS = 384
MLA_Q = 512
MLA_KV = 256
AB_IN = 2368
CD_IN = 4608

ATTN_SUB = 256
TM = 256
TILES = T // TM
LAT_TILES = S // TM
CTX_ROW = B
VMEM_LIMIT = 56 * 1024 * 1024


def _cp(*sem):
    return pltpu.CompilerParams(dimension_semantics=sem, vmem_limit_bytes=VMEM_LIMIT)


def _dot(a, b):
    return jnp.dot(a, b, preferred_element_type=F32)


def _dot_nt(a, b):
    return lax.dot_general(a, b, (((1,), (1,)), ((), ())), preferred_element_type=F32)


def _ada_body(c_ref, w_ref, b_ref, o_ref):
    c = c_ref[...]
    a = c * jax.nn.sigmoid(c)
    o_ref[0] = _dot(a.astype(BF16), w_ref[0].astype(BF16)) + b_ref[0]


def ada_modulation(c8, w_ada, b_ada):
    tn = 1024
    n = 6 * D
    return pl.pallas_call(
        _ada_body,
        grid=(DEPTH, n // tn),
        in_specs=[
            pl.BlockSpec((8, D), lambda l, j: (0, 0)),
            pl.BlockSpec((1, D, tn), lambda l, j: (l, 0, j)),
            pl.BlockSpec((1, 1, tn), lambda l, j: (l, 0, j)),
        ],
        out_specs=pl.BlockSpec((1, 8, tn), lambda l, j: (l, 0, j)),
        out_shape=jax.ShapeDtypeStruct((DEPTH, 8, n), F32),
        compiler_params=_cp("parallel", "parallel"),
        name="ada_modulation",
    )(c8, w_ada, b_ada.reshape(DEPTH, 1, n))


def _rms(x, gain):
    return x * lax.rsqrt(jnp.mean(x * x, axis=-1, keepdims=True) + EPS) * gain


TM_IN = T // 6
TM_BIG = T // 3


def _row_mod(g_ref, tile, tm):
    per = T // tm
    r = (tile % per) * tm + lax.broadcasted_iota(I32, (tm, 1), 0)
    return jnp.where(r >= S, g_ref[0, CTX_ROW:CTX_ROW + 1, :], g_ref[0, pl.ds(tile // per, 1), :])


def _proj_in_body(*refs, n_extra, epilogue):
    x_ref, gain_ref, sc_ref, sh_ref, w_ref = refs[:5]
    tile = pl.program_id(1)
    h = _rms(x_ref[...], gain_ref[...]) * (1.0 + _row_mod(sc_ref, tile, TM_IN)) + _row_mod(sh_ref, tile, TM_IN)
    epilogue(_dot(h.astype(BF16), w_ref[...]), refs[5:5 + n_extra], refs[5 + n_extra:])


def project_in(x2d, gain, modt, l, w, col_blk, tn, n_tiles, epilogue, extras, out_widths, name):
    tabs_per_batch = T // TM_IN
    extra_specs = [pl.BlockSpec((TM_IN, HD), lambda j, i: (i % tabs_per_batch, 0)) if is_table
                   else pl.BlockSpec(a.shape, lambda j, i: (0, 0)) for a, is_table in extras]
    if n_tiles > 1:
        out_specs = [pl.BlockSpec((TM_IN, tn), lambda j, i: (i, j))]
    else:
        out_specs = [pl.BlockSpec((TM_IN, wd), lambda j, i: (i, 0)) for wd in out_widths]
    return pl.pallas_call(
        functools.partial(_proj_in_body, n_extra=len(extras), epilogue=epilogue),
        grid=(n_tiles, R // TM_IN),
        in_specs=[
            pl.BlockSpec((TM_IN, D), lambda j, i: (i, 0)),
            pl.BlockSpec((1, D), lambda j, i: (0, 0)),
            pl.BlockSpec((1, 8, D), lambda j, i: (l, 0, 1)),
            pl.BlockSpec((1, 8, D), lambda j, i: (l, 0, 0)),
            pl.BlockSpec((D, tn), lambda j, i: (0, col_blk + j)),
        ] + extra_specs,
        out_specs=out_specs,
        out_shape=[jax.ShapeDtypeStruct((R, wd), BF16) for wd in out_widths],
        compiler_params=_cp("parallel", "parallel"),
        name=name,
    )(x2d, gain.reshape(1, D), modt, modt, w, *[a for a, _ in extras])


def _mm_res_body(a_ref, b_ref, x_ref, g_ref, o_ref):
    y = _dot(a_ref[...], b_ref[...])
    o_ref[...] = x_ref[...] + _row_mod(g_ref, pl.program_id(1), TM_BIG) * y


def matmul_gated_residual(a, b, x2d, modt, l, chunk):
    tn = 1024
    k = a.shape[1]
    per = D // tn
    return pl.pallas_call(
        _mm_res_body,
        grid=(D // tn, R // TM_BIG),
        in_specs=[
            pl.BlockSpec((TM_BIG, k), lambda j, i: (i, 0)),
            pl.BlockSpec((k, tn), lambda j, i: (0, j)),
            pl.BlockSpec((TM_BIG, tn), lambda j, i: (i, j)),
            pl.BlockSpec((1, 8, tn), lambda j, i: (l, 0, chunk * per + j)),
        ],
        out_specs=pl.BlockSpec((TM_BIG, tn), lambda j, i: (i, j)),
        out_shape=jax.ShapeDtypeStruct((R, D), F32),
        compiler_params=_cp("parallel", "parallel"),
        name="matmul_gated_residual",
    )(a, b, x2d, modt)


def _softmax_pv(parts, sink, rows):
    m = parts[0][0].max(axis=-1, keepdims=True)
    for s, _ in parts[1:]:
        m = jnp.maximum(m, s.max(axis=-1, keepdims=True))
    if sink is not None:
        m = jnp.maximum(m, sink)
    den = jnp.zeros((rows, 1), F32) if sink is None else jnp.exp2(sink - m)
    acc = None
    for s, v in parts:
        p = jnp.exp2(s - m)
        den = den + p.sum(axis=-1, keepdims=True)
        pv = _dot(p.astype(BF16), v)
        acc = pv if acc is None else acc + pv
    return acc / den


def _dense_attn_body(*refs, g, tq, dk, dv, has_sink):
    if has_sink:
        q_ref, k_ref, v_ref, sink_ref, _, o_ref = refs
        sink = sink_ref[0]
    else:
        q_ref, k_ref, v_ref, _, o_ref = refs
        sink = None
    sub = min(tq, ATTN_SUB)
    for j in range(g):
        for r in range(tq // sub):
            rows = slice(r * sub, (r + 1) * sub)
            s = _dot_nt(q_ref[0, rows, j * dk:(j + 1) * dk], k_ref[0])
            sk = None if sink is None else sink[j * tq + r * sub:j * tq + (r + 1) * sub]
            o = _softmax_pv([(s, v_ref[0])], sk, sub)
            o_ref[0, rows, j * dv:(j + 1) * dv] = o.astype(o_ref.dtype)


def dense_attention(att, q, k, v, *, kvh, g, dk, dv, tq, q_off, n_q, nk, k_blk, col_off, sink_col=None,
                    k_col=0, v_col=0):
    in_specs = [
        pl.BlockSpec((1, tq, g * dk), lambda b, h, i: (b, i + q_off, h)),
        pl.BlockSpec((1, nk, dk), lambda b, h, i: (b, k_blk, h + k_col)),
        pl.BlockSpec((1, nk, dv), lambda b, h, i: (b, k_blk, h + v_col)),
    ]
    args = [q, k, v]
    if sink_col is not None:
        in_specs.append(pl.BlockSpec((1, g * tq, 1), lambda b, h, i: (h, 0, 0)))
        args.append(sink_col)
    in_specs.append(pl.BlockSpec(memory_space=pl.ANY))
    args.append(att)
    return pl.pallas_call(
        functools.partial(_dense_attn_body, g=g, tq=tq, dk=dk, dv=dv, has_sink=sink_col is not None),
        grid=(B, kvh, n_q),
        in_specs=in_specs,
        out_specs=pl.BlockSpec((1, tq, g * dv), lambda b, h, i: (b, i + q_off, h + col_off)),
        out_shape=jax.ShapeDtypeStruct(att.shape, att.dtype),
        input_output_aliases={len(args) - 1: 0},
        compiler_params=_cp("parallel", "parallel", "parallel"),
        name="dense_attention",
    )(*args)


WIN_TQ = 512
WIN_SPAN = WIN_TQ + 2 * WINDOW


def _window_attn_body(q_ref, k_ref, v_ref, sink_ref, _, o_ref, *, g):
    q0 = pl.program_id(2) * WIN_TQ
    start = pl.multiple_of(jnp.clip(q0 - WINDOW, 0, S - WIN_SPAN), WINDOW)
    kl = k_ref[0, pl.ds(start, WIN_SPAN), :]
    vl = v_ref[0, pl.ds(start, WIN_SPAN), :]
    qi = lax.broadcasted_iota(I32, (WIN_TQ, WIN_SPAN), 0) + q0
    kj = lax.broadcasted_iota(I32, (WIN_TQ, WIN_SPAN), 1) + start
    valid = jnp.abs(qi - kj) <= WINDOW
    for j in range(g):
        q = q_ref[0, :, j * HD:(j + 1) * HD]
        s_loc = jnp.where(valid, _dot_nt(q, kl), NEG)
        s_ctx = _dot_nt(q, k_ref[0, S:T, :])
        o = _softmax_pv([(s_loc, vl), (s_ctx, v_ref[0, S:T, :])], sink_ref[0, j * WIN_TQ:(j + 1) * WIN_TQ],
                        WIN_TQ)
        o_ref[0, :, j * HD:(j + 1) * HD] = o.astype(o_ref.dtype)


def window_attention(att, q, k, v, sink_col, *, kvh, g):
    tq = WIN_TQ
    return pl.pallas_call(
        functools.partial(_window_attn_body, g=g),
        grid=(B, kvh, S // tq),
        in_specs=[
            pl.BlockSpec((1, tq, g * HD), lambda b, h, m: (b, m, h)),
            pl.BlockSpec((1, T, HD), lambda b, h, m: (b, 0, h)),
            pl.BlockSpec((1, T, HD), lambda b, h, m: (b, 0, h)),
            pl.BlockSpec((1, g * WIN_TQ, 1), lambda b, h, m: (h, 0, 0)),
            pl.BlockSpec(memory_space=pl.ANY),
        ],
        out_specs=pl.BlockSpec((1, tq, g * HD), lambda b, h, m: (b, m, h)),
        out_shape=jax.ShapeDtypeStruct(att.shape, att.dtype),
        input_output_aliases={4: 0},
        compiler_params=_cp("parallel", "parallel", "parallel"),
        name="window_attention",
    )(q, k, v, sink_col, att)


ROWS_N = S // GRID_W
NBR_G = 4
NBR_KR = NBR_G + NB_ROWS
NBR_GROUPS = ROWS_N // NBR_G
NBR_PATTERN_GROUPS = (0, 1, NBR_GROUPS - 1)


def _nbr_window_start(grp):
    return NBR_G * grp - NB_ROWS // 2


def _nbr_bias_body(rpb_ref, o_ref):
    h = pl.program_id(0)
    n_dc = 2 * NB_COLS - 1
    n_dr = 2 * NB_ROWS - 1
    qc = lax.broadcasted_iota(I32, (GRID_W, GRID_W), 0)
    kc = lax.broadcasted_iota(I32, (GRID_W, GRID_W), 1)
    dc = jnp.clip(kc - qc + NB_COLS - 1, 0, n_dc - 1)
    cs = jnp.clip(qc - NB_COLS // 2, 0, GRID_W - NB_COLS)
    col_ok = (kc >= cs) & (kc < cs + NB_COLS)
    tiles = [jnp.zeros((GRID_W, GRID_W), F32) for _ in range(n_dr)]
    for t in range(n_dc):
        hit = dc == t
        for dr in range(n_dr):
            tiles[dr] = jnp.where(hit, rpb_ref[h, dr * n_dc + t] * LOG2E, tiles[dr])
    tiles = [jnp.where(col_ok, x, NEG) for x in tiles]
    masked = jnp.full((GRID_W, GRID_W), NEG, F32)
    for p, grp in enumerate(NBR_PATTERN_GROUPS):
        start = min(max(_nbr_window_start(grp), 0), ROWS_N - NBR_KR)
        for ri in range(NBR_G):
            r = NBR_G * grp + ri
            rs = min(max(r - NB_ROWS // 2, 0), ROWS_N - NB_ROWS)
            row = [tiles[start + kr - r + NB_ROWS - 1] if rs <= start + kr < rs + NB_ROWS else masked
                   for kr in range(NBR_KR)]
            o_ref[0, p, ri * GRID_W:(ri + 1) * GRID_W, :] = jnp.concatenate(row, axis=1)


def nbr_bias_table(rpb):
    heads = rpb.shape[0]
    return pl.pallas_call(
        _nbr_bias_body,
        grid=(heads,),
        in_specs=[pl.BlockSpec(memory_space=pltpu.SMEM)],
        out_specs=pl.BlockSpec((1, 3, NBR_G * GRID_W, NBR_KR * GRID_W), lambda h: (h, 0, 0, 0)),
        out_shape=jax.ShapeDtypeStruct((heads, 3, NBR_G * GRID_W, NBR_KR * GRID_W), F32),
        compiler_params=_cp("parallel"),
        name="nbr_bias_table",
    )(rpb.astype(F32).reshape(heads, -1))


NBR_STEP = 4


def _nbr_attn_body(q_ref, k_ref, v_ref, bias_ref, _, o_ref):
    n_loc = NBR_KR * GRID_W
    rows_g = NBR_G * GRID_W
    for w in range(NBR_STEP):
        grp = pl.program_id(2) * NBR_STEP + w
        pattern = jnp.where(grp == 0, 0, jnp.where(grp == NBR_GROUPS - 1, 2, 1))
        start_row = jnp.clip(_nbr_window_start(grp), 0, ROWS_N - NBR_KR)
        start = pl.multiple_of(start_row * GRID_W, GRID_W)
        rows = slice(w * rows_g, (w + 1) * rows_g)
        q = q_ref[0, rows, :]
        s_loc = _dot_nt(q, k_ref[0, pl.ds(start, n_loc), :]) + bias_ref[0, pattern]
        s_ctx = _dot_nt(q, k_ref[0, S:T, :])
        o = _softmax_pv([(s_loc, v_ref[0, pl.ds(start, n_loc), :]), (s_ctx, v_ref[0, S:T, :])], None, rows_g)
        o_ref[0, rows, :] = o.astype(o_ref.dtype)


def neighbourhood_attention(att, q, k, v, bias_tab, *, heads, col_off, k_col=0, v_col=0):
    tq = NBR_STEP * NBR_G * GRID_W
    return pl.pallas_call(
        _nbr_attn_body,
        grid=(B, heads, NBR_GROUPS // NBR_STEP),
        in_specs=[
            pl.BlockSpec((1, tq, HD), lambda b, h, r: (b, r, h)),
            pl.BlockSpec((1, T, HD), lambda b, h, r: (b, 0, h + k_col)),
            pl.BlockSpec((1, T, HD), lambda b, h, r: (b, 0, h + v_col)),
            pl.BlockSpec((1, 3, NBR_G * GRID_W, NBR_KR * GRID_W), lambda b, h, r: (h, 0, 0, 0)),
            pl.BlockSpec(memory_space=pl.ANY),
        ],
        out_specs=pl.BlockSpec((1, tq, HD), lambda b, h, r: (b, r, h + col_off)),
        out_shape=jax.ShapeDtypeStruct(att.shape, att.dtype),
        input_output_aliases={4: 0},
        compiler_params=_cp("parallel", "parallel", "parallel"),
        name="neighbourhood_attention",
    )(q, k, v, bias_tab, att)


def _cumsum_lanes(mask, tri):
    n = mask.shape[1]
    carry = jnp.zeros((mask.shape[0], 1), F32)
    out = []
    for j in range(n // 256):
        blk = jnp.where(mask[:, j * 256:(j + 1) * 256], 1.0, 0.0).astype(BF16)
        c = _dot(blk, tri) + carry
        carry = c[:, 255:256]
        out.append(c)
    return jnp.concatenate(out, axis=1) if len(out) > 1 else out[0]


def _count(mask):
    return jnp.sum(jnp.where(mask, 1.0, 0.0), axis=1, keepdims=True)


def _topk_slots(affs, caps, tri):
    rows = affs[0].shape[0]

    def step(carry):
        bounds, _ = carry
        new, moved = [], []
        for aff, cap, (lo, hi) in zip(affs, caps, bounds):
            mid = 0.5 * (lo + hi)
            ok = _count(aff >= mid) >= cap
            nlo = jnp.where(ok, mid, lo)
            nhi = jnp.where(ok, hi, mid)
            new.append((nlo, nhi))
            moved.append(jnp.where((nlo != lo) | (nhi != hi), 1.0, 0.0))
        return tuple(new), jnp.max(functools.reduce(jnp.maximum, moved))

    start = tuple((jnp.zeros((rows, 1), F32), jnp.full((rows, 1), 2.0, F32)) for _ in affs)
    bounds, _ = lax.while_loop(lambda c: c[1] > 0.0, step, (start, jnp.float32(1.0)))
    slots = []
    for aff, cap, (lo, hi) in zip(affs, caps, bounds):
        gt = aff >= hi
        eq = (aff >= lo) & (aff < hi)
        need = cap - _count(gt)
        sel = gt | (eq & (_cumsum_lanes(eq, tri) <= need))
        pos = _cumsum_lanes(sel, tri) - 1.0
        slots.append(jnp.where(sel, pos.astype(I32), -1))
    return slots


def _router_body(x_ref, gain_ref, sc_ref, sh_ref, wr_ref, h_ref, slot_ref, aff_ref, lg_scr):
    b = pl.program_id(0)
    tt = pl.program_id(1)
    row = jnp.where(tt == LAT_TILES, CTX_ROW, b)
    y = _rms(x_ref[...], gain_ref[...])
    h = y * (1.0 + sc_ref[0, pl.ds(row, 1), :]) + sh_ref[0, pl.ds(row, 1), :]
    h_ref[...] = h.astype(BF16)
    w = wr_ref[...]
    w_hi, h_hi = w.astype(BF16), h.astype(BF16)
    w_lo = (w - w_hi.astype(F32)).astype(BF16)
    h_lo = (h - h_hi.astype(F32)).astype(BF16)
    lg_scr[tt] = _dot_nt(w_hi, h_hi) + (_dot_nt(w_hi, h_lo) + _dot_nt(w_lo, h_hi))

    @pl.when(tt == TILES - 1)
    def _():
        tri = jnp.where(lax.broadcasted_iota(I32, (256, 256), 0) <= lax.broadcasted_iota(I32, (256, 256), 1),
                        1.0, 0.0).astype(BF16)
        lat = jnp.concatenate([lg_scr[j] for j in range(LAT_TILES)], axis=1)
        affs = []
        for lg in (lat, lg_scr[LAT_TILES]):
            ex = jnp.exp(lg - lg.max(axis=0, keepdims=True))
            affs.append(ex / ex.sum(axis=0, keepdims=True))
        slots = _topk_slots(affs, (CAP_L, CAP_C), tri)
        for (off, n, _, _), aff, slot in zip(_SETS, affs, slots):
            slot_ref[0, :, off:off + n] = slot
            aff_ref[0, :, off:off + n] = aff


def router(x2d, gain, modt, l, w_router_t):
    return pl.pallas_call(
        _router_body,
        grid=(B, TILES),
        in_specs=[
            pl.BlockSpec((TM, D), lambda b, t: (b * TILES + t, 0)),
            pl.BlockSpec((1, D), lambda b, t: (0, 0)),
            pl.BlockSpec((1, 8, D), lambda b, t: (l, 0, 4)),
            pl.BlockSpec((1, 8, D), lambda b, t: (l, 0, 3)),
            pl.BlockSpec((E, D), lambda b, t: (0, 0)),
        ],
        out_specs=[
            pl.BlockSpec((TM, D), lambda b, t: (b * TILES + t, 0)),
            pl.BlockSpec((1, E, T), lambda b, t: (b, 0, 0)),
            pl.BlockSpec((1, E, T), lambda b, t: (b, 0, 0)),
        ],
        out_shape=[
            jax.ShapeDtypeStruct((R, D), BF16),
            jax.ShapeDtypeStruct((B, E, T), I32),
            jax.ShapeDtypeStruct((B, E, T), F32),
        ],
        scratch_shapes=[pltpu.VMEM((TILES, E, TM), F32)],
        compiler_params=_cp("parallel", "arbitrary"),
        name="router",
    )(x2d, gain.reshape(1, D), modt, modt, w_router_t)


_SETS = ((0, S, CAP_L, 0), (S, L, CAP_C, CAP_L))


def _dispatch_body(h_ref, slot_ref, aff_ref, xin_ref, gate_ref):
    e = pl.program_id(1)
    srow = slot_ref[0, pl.ds(e, 1), :]
    arow = aff_ref[0, pl.ds(e, 1), :]
    for off, n, cap, o0 in _SETS:
        hit = lax.broadcasted_iota(I32, (cap, n), 0) == srow[:, off:off + n]
        onehot = jnp.where(hit, 1.0, 0.0).astype(BF16)
        xin_ref[0, 0, o0:o0 + cap, :] = _dot(onehot, h_ref[0, off:off + n, :]).astype(BF16)
        gate_ref[0, 0, o0:o0 + cap, :] = jnp.sum(jnp.where(hit, arow[:, off:off + n], 0.0), axis=1, keepdims=True)


def dispatch(h2, slot, aff):
    return pl.pallas_call(
        _dispatch_body,
        grid=(B, E),
        in_specs=[
            pl.BlockSpec((1, T, D), lambda b, e: (b, 0, 0)),
            pl.BlockSpec((1, E, T), lambda b, e: (b, 0, 0)),
            pl.BlockSpec((1, E, T), lambda b, e: (b, 0, 0)),
        ],
        out_specs=[
            pl.BlockSpec((1, 1, SLOTS, D), lambda b, e: (e, b, 0, 0)),
            pl.BlockSpec((1, 1, SLOTS, 1), lambda b, e: (e, b, 0, 0)),
        ],
        out_shape=[
            jax.ShapeDtypeStruct((E, B, SLOTS, D), BF16),
            jax.ShapeDtypeStruct((E, B, SLOTS, 1), F32),
        ],
        compiler_params=_cp("parallel", "parallel"),
        name="moe_dispatch",
    )(h2.reshape(B, T, D), slot, aff)


def _ffn_body(x_ref, wg_ref, wu_ref, wd_ref, gate_ref, o_ref, acc_ref):
    f = pl.program_id(1)
    wg = wg_ref[0, 0].astype(BF16)
    wu = wu_ref[0, 0].astype(BF16)
    wd = wd_ref[0, 0].astype(BF16)

    @pl.when(f == 0)
    def _():
        acc_ref[...] = jnp.zeros_like(acc_ref)

    for r in range(x_ref.shape[1] // FFN_ROWS):
        rows = slice(r * FFN_ROWS, (r + 1) * FFN_ROWS)
        x = x_ref[0, rows, :]
        a = _dot(x, wg)
        u = _dot(x, wu)
        hid = (a * jax.nn.sigmoid(a) * u).astype(BF16)
        acc_ref[rows, :] += _dot(hid, wd)

    @pl.when(f == pl.num_programs(1) - 1)
    def _():
        o_ref[0] = (acc_ref[...] * gate_ref[0]).astype(o_ref.dtype)


def expert_ffn(xin, gates, w_gate, w_up, w_down, l):
    tf = 256
    m = B * SLOTS
    return pl.pallas_call(
        _ffn_body,
        grid=(E, FE // tf),
        in_specs=[
            pl.BlockSpec((1, m, D), lambda e, f: (e, 0, 0)),
            pl.BlockSpec((1, 1, D, tf), lambda e, f: (l, e, 0, f)),
            pl.BlockSpec((1, 1, D, tf), lambda e, f: (l, e, 0, f)),
            pl.BlockSpec((1, 1, tf, D), lambda e, f: (l, e, f, 0)),
            pl.BlockSpec((1, m, 1), lambda e, f: (e, 0, 0)),
        ],
        out_specs=pl.BlockSpec((1, m, D), lambda e, f: (e, 0, 0)),
        out_shape=jax.ShapeDtypeStruct((E, m, D), BF16),
        scratch_shapes=[pltpu.VMEM((m, D), F32)],
        compiler_params=_cp("parallel", "arbitrary"),
        name="expert_ffn",
    )(xin.reshape(E, m, D), w_gate, w_up, w_down, gates.reshape(E, m, 1))


def _combine_body(o_ref, slot_ref, x_ref, g_ref, out_ref):
    b = pl.program_id(0)
    tt = pl.program_id(2)
    tn = out_ref.shape[1]

    def run(cap, o0, row):
        acc = jnp.zeros((TM, tn), F32)
        sl = lax.broadcasted_iota(I32, (TM, cap), 1)
        for e in range(E):
            onehot = jnp.where(slot_ref[0, :, e:e + 1] == sl, 1.0, 0.0).astype(BF16)
            acc = acc + _dot(onehot, o_ref[e, 0, o0:o0 + cap, :])
        out_ref[...] = x_ref[...] + g_ref[0, pl.ds(row, 1), :] * acc

    @pl.when(tt < LAT_TILES)
    def _():
        run(CAP_L, 0, b)

    @pl.when(tt == LAT_TILES)
    def _():
        run(CAP_C, CAP_L, CTX_ROW)


def combine(o, slot_tok, x2d, modt, l):
    tn = 1024
    per = D // tn
    return pl.pallas_call(
        _combine_body,
        grid=(B, per, TILES),
        in_specs=[
            pl.BlockSpec((E, 1, SLOTS, tn), lambda b, j, t: (0, b, 0, j)),
            pl.BlockSpec((1, TM, E), lambda b, j, t: (b, t, 0)),
            pl.BlockSpec((TM, tn), lambda b, j, t: (b * TILES + t, j)),
            pl.BlockSpec((1, 8, tn), lambda b, j, t: (l, 0, 5 * per + j)),
        ],
        out_specs=pl.BlockSpec((TM, tn), lambda b, j, t: (b * TILES + t, j)),
        out_shape=jax.ShapeDtypeStruct((R, D), F32),
        compiler_params=_cp("parallel", "parallel", "parallel"),
        name="moe_combine",
    )(o.reshape(E, B, SLOTS, D), slot_tok, x2d, modt)


def moe_block(x2d, gain, modt, l, w_router_t, w_gate, w_up, w_down):
    h2, slot, aff = router(x2d, gain, modt, l, w_router_t)
    xin, gates = dispatch(h2, slot, aff)
    o = expert_ffn(xin, gates, w_gate, w_up, w_down, l)
    return combine(o, jnp.swapaxes(slot, 1, 2), x2d, modt, l)


def _rope_tables(d):
    half = d // 2
    t = jnp.arange(S)
    inv = ROPE_BASE ** (-jnp.arange(0, half, 2, dtype=F32) / half)
    parts_c, parts_s = [], []
    for pos in (t // GRID_W, t % GRID_W):
        ang = pos.astype(F32)[:, None] * inv[None, :]
        parts_c += [jnp.cos(ang), jnp.cos(ang)]
        parts_s += [-jnp.sin(ang), jnp.sin(ang)]
    cos = jnp.concatenate([jnp.concatenate(parts_c, axis=1), jnp.ones((L, d), F32)], axis=0)
    sin = jnp.concatenate([jnp.concatenate(parts_s, axis=1), jnp.zeros((L, d), F32)], axis=0)
    return cos, sin


def _rope_lanes(x, cos, sin, quarter):
    lane = lax.broadcasted_iota(I32, x.shape, 1)
    sw = jnp.where(lane % (2 * quarter) < quarter, pltpu.roll(x, HD - quarter, 1), pltpu.roll(x, quarter, 1))
    return x * cos + sw * sin


PROJ_A = 1536


def _ab_attn_epilogue(y, extras, outs):
    (c_ref, s_ref), (aq_ref, ak_ref, av_ref) = extras, outs
    cos, sin = c_ref[...], s_ref[...]
    for h in range(8):
        aq = _rope_lanes(y[:, h * HD:(h + 1) * HD], cos, sin, 32) * QK_SCALE_LOG2
        aq_ref[:, h * HD:(h + 1) * HD] = aq.astype(BF16)
    for h in range(2):
        ak_ref[:, h * HD:(h + 1) * HD] = _rope_lanes(y[:, 1024 + h * HD:1024 + (h + 1) * HD], cos, sin,
                                                     32).astype(BF16)
    av_ref[...] = y[:, 1280:PROJ_A].astype(BF16)


def _ab_mla_epilogue(y, extras, outs):
    (c64p_ref, s64p_ref, c64x2_ref, s64x2_ref, qn_ref, kvn_ref, wq_ref, wkv_ref), (q_ref, k_ref, v_ref) = extras, outs
    cq = _rms(y[:, :MLA_Q], qn_ref[...]).astype(BF16)
    ckv = _rms(y[:, MLA_Q:MLA_Q + MLA_KV], kvn_ref[...]).astype(BF16)
    zeros = jnp.zeros((TM_IN, 64), F32)
    kr = jnp.concatenate([y[:, MLA_Q + MLA_KV:], zeros], axis=1)
    kr = _rope_lanes(kr, c64p_ref[...], s64p_ref[...], 16).astype(BF16)

    yq = _dot(cq, wq_ref[...])
    cos, sin = c64x2_ref[...], s64x2_ref[...]
    for j in range(4):
        rot = _rope_lanes(yq[:, 1024 + j * HD:1024 + (j + 1) * HD], cos, sin, 16)
        for u in range(2):
            h = 2 * j + u
            q_ref[:, 256 * h:256 * h + HD] = (yq[:, h * HD:(h + 1) * HD] * MLA_SCALE_LOG2).astype(BF16)
            q_ref[:, 256 * h + HD:256 * (h + 1)] = jnp.concatenate(
                [rot[:, 64 * u:64 * (u + 1)] * MLA_SCALE_LOG2, zeros], axis=1).astype(BF16)

    ykv = _dot(ckv, wkv_ref[...])
    for h in range(8):
        k_ref[:, 256 * h:256 * h + HD] = ykv[:, h * HD:(h + 1) * HD].astype(BF16)
        k_ref[:, 256 * h + HD:256 * (h + 1)] = kr
    v_ref[...] = ykv[:, 1024:].astype(BF16)


def _cd_attn_epilogue(y, extras, outs):
    (c_ref, s_ref, qn_ref, kn_ref), (cq_ref, ck_ref, cv_ref) = extras, outs
    cos, sin = c_ref[...], s_ref[...]
    for h in range(8):
        x = _rms(y[:, h * HD:(h + 1) * HD], qn_ref[...])
        cq_ref[:, h * HD:(h + 1) * HD] = (_rope_lanes(x, cos, sin, 32) * QK_SCALE_LOG2).astype(BF16)
    for h in range(2):
        x = _rms(y[:, 1024 + h * HD:1024 + (h + 1) * HD], kn_ref[...])
        ck_ref[:, h * HD:(h + 1) * HD] = _rope_lanes(x, cos, sin, 32).astype(BF16)
    cv_ref[...] = y[:, 1280:PROJ_A].astype(BF16)


def _cd_nbr_epilogue(y, extras, outs):
    j = pl.program_id(0)
    outs[0][...] = (y * extras[0][pl.ds(j, 1), :]).astype(BF16)


def _sink_col(sink, kvh, g, tq):
    col = (sink.astype(F32) * LOG2E).reshape(kvh, g, 1, 1)
    return jnp.broadcast_to(col, (kvh, g, tq, 1)).reshape(kvh, g * tq, 1)


def _split_heads(w, n_first):
    k = w.shape[0]
    w3 = w.reshape(k, 8, -1)
    return jnp.concatenate([w3[:, :, :n_first].reshape(k, -1), w3[:, :, n_first:].reshape(k, -1)], axis=1)


def mixer_ab(x2d, gain, modt, l, need_ctx, w_in, sink, q_norm, w_uq, kv_norm, w_ukv, tabs):
    w_in = w_in.astype(BF16)
    table = [(t, True) for t in tabs["r128"]]
    aq, ak, av = project_in(x2d, gain, modt, l, w_in, 0, PROJ_A, 1, _ab_attn_epilogue, table,
                            (1024, 256, 256), "project_in_ab_attn")
    extras = [(t, True) for t in tabs["r64pad"] + tabs["r64x2"]] + [
        (q_norm.reshape(1, MLA_Q), False), (kv_norm.reshape(1, MLA_KV), False),
        (_split_heads(w_uq, 128).astype(BF16), False), (_split_heads(w_ukv, 128).astype(BF16), False)]
    bq, bk, bv = project_in(x2d, gain, modt, l, w_in[:, PROJ_A:], 0, AB_IN - PROJ_A, 1, _ab_mla_epilogue, extras,
                            (2048, 2048, 1024), "project_in_ab_mla")
    aq, ak, av, bq, bk, bv = (t.reshape(B, T, -1) for t in (aq, ak, av, bq, bk, bv))

    ctx_blk = S // L
    att = jnp.zeros((B, T, D), BF16)
    att = window_attention(att, aq, ak, av, _sink_col(sink, 2, 4, WIN_TQ), kvh=2, g=4)
    att = dense_attention(att, bq, bk, bv, kvh=8, g=1, dk=256, dv=HD, tq=1024, q_off=0, n_q=S // 1024, nk=T,
                          k_blk=0, col_off=8)
    if need_ctx:
        att = dense_attention(att, aq, ak, av, kvh=2, g=4, dk=HD, dv=HD, tq=L, q_off=ctx_blk, n_q=1, nk=L,
                              k_blk=ctx_blk, col_off=0, sink_col=_sink_col(sink, 2, 4, L))
        att = dense_attention(att, bq, bk, bv, kvh=8, g=1, dk=256, dv=HD, tq=L, q_off=ctx_blk, n_q=1, nk=L,
                              k_blk=ctx_blk, col_off=8)
    return att.reshape(R, D)


def mixer_cd(x2d, gain, modt, l, need_ctx, w_in, q_norm, k_norm, rpb, tabs):
    w_in = w_in.astype(BF16)
    extras = [(t, True) for t in tabs["r128"]] + [(q_norm.reshape(1, HD), False), (k_norm.reshape(1, HD), False)]
    cq, ck, cv = (t.reshape(B, T, -1) for t in project_in(
        x2d, gain, modt, l, w_in, 0, PROJ_A, 1, _cd_attn_epilogue, extras, (1024, 256, 256), "project_in_cd_attn"))
    n_d = CD_IN - PROJ_A
    q_scale = jnp.concatenate([jnp.full((1024,), QK_SCALE_LOG2, F32), jnp.ones((n_d - 1024,), F32)])
    dqkv, = project_in(x2d, gain, modt, l, w_in, 1, PROJ_A, n_d // PROJ_A, _cd_nbr_epilogue,
                       [(q_scale.reshape(n_d // PROJ_A, PROJ_A), False)], (n_d,), "project_in_cd_nbr")
    dqkv = dqkv.reshape(B, T, n_d)
    ctx_blk = S // L
    att = jnp.zeros((B, T, D), BF16)
    att = dense_attention(att, cq, ck, cv, kvh=2, g=4, dk=HD, dv=HD, tq=512, q_off=0, n_q=S // 512, nk=T, k_blk=0,
                          col_off=0)
    att = neighbourhood_attention(att, dqkv, dqkv, dqkv, nbr_bias_table(rpb), heads=8, col_off=8, k_col=8,
                                  v_col=16)
    if need_ctx:
        att = dense_attention(att, cq, ck, cv, kvh=2, g=4, dk=HD, dv=HD, tq=L, q_off=ctx_blk, n_q=1, nk=L,
                              k_blk=ctx_blk, col_off=0)
        att = dense_attention(att, dqkv, dqkv, dqkv, kvh=8, g=1, dk=HD, dv=HD, tq=L, q_off=ctx_blk, n_q=1, nk=L,
                              k_blk=ctx_blk, col_off=8, k_col=8, v_col=16)
    return att.reshape(R, D)


def _final_norm_body(x_ref, g_ref, o_ref):
    o_ref[0] = _rms(x_ref[0], g_ref[...])


def final_rmsnorm(x3, gain):
    return pl.pallas_call(
        _final_norm_body,
        grid=(B, S // TM),
        in_specs=[pl.BlockSpec((1, TM, D), lambda b, i: (b, i, 0)), pl.BlockSpec((1, D), lambda b, i: (0, 0))],
        out_specs=pl.BlockSpec((1, TM, D), lambda b, i: (b, i, 0)),
        out_shape=jax.ShapeDtypeStruct((B, S, D), F32),
        compiler_params=_cp("parallel", "parallel"),
        name="final_norm",
    )(x3, gain.reshape(1, D))


def kernel(x, c, ctx, c_ctx, w_ada, b_ada, norm1, norm2, ab_w_in, ab_sink, ab_q_norm, ab_w_uq, ab_kv_norm,
           ab_w_ukv, ab_w_out, cd_w_in, cd_q_norm, cd_k_norm, cd_rpb, cd_w_out, w_router, w_gate, w_up, w_down,
           final_norm):
    c8 = jnp.concatenate([c, c_ctx[None, :], jnp.zeros((8 - B - 1, D), F32)], axis=0)
    modt = ada_modulation(c8, w_ada, b_ada)
    c64, s64 = _rope_tables(64)
    zeros64 = jnp.zeros((T, 64), F32)
    tabs = {
        "r128": _rope_tables(HD),
        "r64x2": (jnp.concatenate([c64, c64], axis=1), jnp.concatenate([s64, s64], axis=1)),
        "r64pad": (jnp.concatenate([c64, zeros64], axis=1), jnp.concatenate([s64, zeros64], axis=1)),
    }
    x2d = jnp.concatenate([x, ctx], axis=1).reshape(R, D)
    for l in range(DEPTH):
        need_ctx = l < DEPTH - 1
        i = l // 2
        if l % 2 == 0:
            att = mixer_ab(x2d, norm1[l], modt, l, need_ctx, ab_w_in[i], ab_sink[i], ab_q_norm[i], ab_w_uq[i],
                           ab_kv_norm[i], ab_w_ukv[i], tabs)
            w_out = ab_w_out[i]
        else:
            att = mixer_cd(x2d, norm1[l], modt, l, need_ctx, cd_w_in[i], cd_q_norm[i], cd_k_norm[i], cd_rpb[i],
                           tabs)
            w_out = cd_w_out[i]
        x2d = matmul_gated_residual(att, w_out.astype(BF16), x2d, modt, l, 2)
        x2d = moe_block(x2d, norm2[l], modt, l, w_router[l].T, w_gate, w_up, w_down)
    return final_rmsnorm(x2d.reshape(B, T, D), final_norm)
```

```python
import functools

import jax
import jax.numpy as jnp
from jax import lax
from jax.experimental import pallas as pl
from jax.experimental.pallas import tpu as pltpu

F32 = jnp.float32
BF16 = jnp.bfloat16
I32 = jnp.int32

D = 2048
B = 4
S = 2048
L = 256
T = S + L
R = B * T
DEPTH = 4
GRID_W = 64
HD = 128
ROPE_BASE = 10000.0
EPS = 1e-6
NEG = -1e30
LOG2E = 1.4426950408889634
QK_SCALE_LOG2 = HD ** -0.5 * LOG2E
MLA_SCALE_LOG2 = 192 ** -0.5 * LOG2E
WINDOW = 128
BQ = 128
NB_ROWS = 8
NB_COLS = 16
E = 16
CAP_L = 2 * S // E
CAP_C = 2 * L // E
SLOTS = CAP_L + CAP_C
FE = 1024
FFN_ROWS = 384
MLA_Q = 512
MLA_KV = 256
AB_IN = 2368
CD_IN = 4608

ATTN_SUB = 256
TM = 256
TILES = T // TM
LAT_TILES = S // TM
CTX_ROW = B
VMEM_LIMIT = 56 * 1024 * 1024


def _cp(*sem):
    return pltpu.CompilerParams(dimension_semantics=sem, vmem_limit_bytes=VMEM_LIMIT)


def _dot(a, b):
    return jnp.dot(a, b, preferred_element_type=F32)


def _dot_nt(a, b):
    return lax.dot_general(a, b, (((1,), (1,)), ((), ())), preferred_element_type=F32)


def _ada_body(c_ref, w_ref, b_ref, o_ref):
    c = c_ref[...]
    a = c * jax.nn.sigmoid(c)
    o_ref[0] = _dot(a.astype(BF16), w_ref[0].astype(BF16)) + b_ref[0]


def ada_modulation(c8, w_ada, b_ada):
    tn = 1024
    n = 6 * D
    return pl.pallas_call(
        _ada_body,
        grid=(DEPTH, n // tn),
        in_specs=[
            pl.BlockSpec((8, D), lambda l, j: (0, 0)),
            pl.BlockSpec((1, D, tn), lambda l, j: (l, 0, j)),
            pl.BlockSpec((1, 1, tn), lambda l, j: (l, 0, j)),
        ],
        out_specs=pl.BlockSpec((1, 8, tn), lambda l, j: (l, 0, j)),
        out_shape=jax.ShapeDtypeStruct((DEPTH, 8, n), F32),
        compiler_params=_cp("parallel", "parallel"),
        name="ada_modulation",
    )(c8, w_ada, b_ada.reshape(DEPTH, 1, n))


def _rms(x, gain):
    return x * lax.rsqrt(jnp.mean(x * x, axis=-1, keepdims=True) + EPS) * gain


TM_IN = T // 6
TM_BIG = T // 3


def _row_mod(g_ref, tile, tm):
    per = T // tm
    r = (tile % per) * tm + lax.broadcasted_iota(I32, (tm, 1), 0)
    return jnp.where(r >= S, g_ref[0, CTX_ROW:CTX_ROW + 1, :], g_ref[0, pl.ds(tile // per, 1), :])


def _proj_in_body(*refs, n_extra, epilogue):
    x_ref, gain_ref, sc_ref, sh_ref, w_ref = refs[:5]
    tile = pl.program_id(1)
    h = _rms(x_ref[...], gain_ref[...]) * (1.0 + _row_mod(sc_ref, tile, TM_IN)) + _row_mod(sh_ref, tile, TM_IN)
    epilogue(_dot(h.astype(BF16), w_ref[...]), refs[5:5 + n_extra], refs[5 + n_extra:])


def project_in(x2d, gain, modt, l, w, col_blk, tn, n_tiles, epilogue, extras, out_widths, name):
    tabs_per_batch = T // TM_IN
    extra_specs = [pl.BlockSpec((TM_IN, HD), lambda j, i: (i % tabs_per_batch, 0)) if is_table
                   else pl.BlockSpec(a.shape, lambda j, i: (0, 0)) for a, is_table in extras]
    if n_tiles > 1:
        out_specs = [pl.BlockSpec((TM_IN, tn), lambda j, i: (i, j))]
    else:
        out_specs = [pl.BlockSpec((TM_IN, wd), lambda j, i: (i, 0)) for wd in out_widths]
    return pl.pallas_call(
        functools.partial(_proj_in_body, n_extra=len(extras), epilogue=epilogue),
        grid=(n_tiles, R // TM_IN),
        in_specs=[
            pl.BlockSpec((TM_IN, D), lambda j, i: (i, 0)),
            pl.BlockSpec((1, D), lambda j, i: (0, 0)),
            pl.BlockSpec((1, 8, D), lambda j, i: (l, 0, 1)),
            pl.BlockSpec((1, 8, D), lambda j, i: (l, 0, 0)),
            pl.BlockSpec((D, tn), lambda j, i: (0, col_blk + j)),
        ] + extra_specs,
        out_specs=out_specs,
        out_shape=[jax.ShapeDtypeStruct((R, wd), BF16) for wd in out_widths],
        compiler_params=_cp("parallel", "parallel"),
        name=name,
    )(x2d, gain.reshape(1, D), modt, modt, w, *[a for a, _ in extras])


def _mm_res_body(a_ref, b_ref, x_ref, g_ref, o_ref):
    y = _dot(a_ref[...], b_ref[...])
    o_ref[...] = x_ref[...] + _row_mod(g_ref, pl.program_id(1), TM_BIG) * y


def matmul_gated_residual(a, b, x2d, modt, l, chunk):
    tn = 1024
    k = a.shape[1]
    per = D // tn
    return pl.pallas_call(
        _mm_res_body,
        grid=(D // tn, R // TM_BIG),
        in_specs=[
            pl.BlockSpec((TM_BIG, k), lambda j, i: (i, 0)),
            pl.BlockSpec((k, tn), lambda j, i: (0, j)),
            pl.BlockSpec((TM_BIG, tn), lambda j, i: (i, j)),
            pl.BlockSpec((1, 8, tn), lambda j, i: (l, 0, chunk * per + j)),
        ],
        out_specs=pl.BlockSpec((TM_BIG, tn), lambda j, i: (i, j)),
        out_shape=jax.ShapeDtypeStruct((R, D), F32),
        compiler_params=_cp("parallel", "parallel"),
        name="matmul_gated_residual",
    )(a, b, x2d, modt)


def _softmax_pv(parts, sink, rows):
    m = parts[0][0].max(axis=-1, keepdims=True)
    for s, _ in parts[1:]:
        m = jnp.maximum(m, s.max(axis=-1, keepdims=True))
    if sink is not None:
        m = jnp.maximum(m, sink)
    den = jnp.zeros((rows, 1), F32) if sink is None else jnp.exp2(sink - m)
    acc = None
    for s, v in parts:
        p = jnp.exp2(s - m)
        den = den + p.sum(axis=-1, keepdims=True)
        pv = _dot(p.astype(BF16), v)
        acc = pv if acc is None else acc + pv
    return acc / den


def _dense_attn_body(*refs, g, tq, dk, dv, has_sink):
    if has_sink:
        q_ref, k_ref, v_ref, sink_ref, _, o_ref = refs
        sink = sink_ref[0]
    else:
        q_ref, k_ref, v_ref, _, o_ref = refs
        sink = None
    sub = min(tq, ATTN_SUB)
    for j in range(g):
        for r in range(tq // sub):
            rows = slice(r * sub, (r + 1) * sub)
            s = _dot_nt(q_ref[0, rows, j * dk:(j + 1) * dk], k_ref[0])
            sk = None if sink is None else sink[j * tq + r * sub:j * tq + (r + 1) * sub]
            o = _softmax_pv([(s, v_ref[0])], sk, sub)
            o_ref[0, rows, j * dv:(j + 1) * dv] = o.astype(o_ref.dtype)


def dense_attention(att, q, k, v, *, kvh, g, dk, dv, tq, q_off, n_q, nk, k_blk, col_off, sink_col=None,
                    k_col=0, v_col=0):
    in_specs = [
        pl.BlockSpec((1, tq, g * dk), lambda b, h, i: (b, i + q_off, h)),
        pl.BlockSpec((1, nk, dk), lambda b, h, i: (b, k_blk, h + k_col)),
        pl.BlockSpec((1, nk, dv), lambda b, h, i: (b, k_blk, h + v_col)),
    ]
    args = [q, k, v]
    if sink_col is not None:
        in_specs.append(pl.BlockSpec((1, g * tq, 1), lambda b, h, i: (h, 0, 0)))
        args.append(sink_col)
    in_specs.append(pl.BlockSpec(memory_space=pl.ANY))
    args.append(att)
    return pl.pallas_call(
        functools.partial(_dense_attn_body, g=g, tq=tq, dk=dk, dv=dv, has_sink=sink_col is not None),
        grid=(B, kvh, n_q),
        in_specs=in_specs,
        out_specs=pl.BlockSpec((1, tq, g * dv), lambda b, h, i: (b, i + q_off, h + col_off)),
        out_shape=jax.ShapeDtypeStruct(att.shape, att.dtype),
        input_output_aliases={len(args) - 1: 0},
        compiler_params=_cp("parallel", "parallel", "parallel"),
        name="dense_attention",
    )(*args)


WIN_TQ = 512
WIN_SPAN = WIN_TQ + 2 * WINDOW


def _window_attn_body(q_ref, k_ref, v_ref, sink_ref, _, o_ref, *, g):
    q0 = pl.program_id(2) * WIN_TQ
    start = pl.multiple_of(jnp.clip(q0 - WINDOW, 0, S - WIN_SPAN), WINDOW)
    kl = k_ref[0, pl.ds(start, WIN_SPAN), :]
    vl = v_ref[0, pl.ds(start, WIN_SPAN), :]
    qi = lax.broadcasted_iota(I32, (WIN_TQ, WIN_SPAN), 0) + q0
    kj = lax.broadcasted_iota(I32, (WIN_TQ, WIN_SPAN), 1) + start
    valid = jnp.abs(qi - kj) <= WINDOW
    for j in range(g):
        q = q_ref[0, :, j * HD:(j + 1) * HD]
        s_loc = jnp.where(valid, _dot_nt(q, kl), NEG)
        s_ctx = _dot_nt(q, k_ref[0, S:T, :])
        o = _softmax_pv([(s_loc, vl), (s_ctx, v_ref[0, S:T, :])], sink_ref[0, j * WIN_TQ:(j + 1) * WIN_TQ],
                        WIN_TQ)
        o_ref[0, :, j * HD:(j + 1) * HD] = o.astype(o_ref.dtype)


def window_attention(att, q, k, v, sink_col, *, kvh, g):
    tq = WIN_TQ
    return pl.pallas_call(
        functools.partial(_window_attn_body, g=g),
        grid=(B, kvh, S // tq),
        in_specs=[
            pl.BlockSpec((1, tq, g * HD), lambda b, h, m: (b, m, h)),
            pl.BlockSpec((1, T, HD), lambda b, h, m: (b, 0, h)),
            pl.BlockSpec((1, T, HD), lambda b, h, m: (b, 0, h)),
            pl.BlockSpec((1, g * WIN_TQ, 1), lambda b, h, m: (h, 0, 0)),
            pl.BlockSpec(memory_space=pl.ANY),
        ],
        out_specs=pl.BlockSpec((1, tq, g * HD), lambda b, h, m: (b, m, h)),
        out_shape=jax.ShapeDtypeStruct(att.shape, att.dtype),
        input_output_aliases={4: 0},
        compiler_params=_cp("parallel", "parallel", "parallel"),
        name="window_attention",
    )(q, k, v, sink_col, att)


ROWS_N = S // GRID_W
NBR_G = 4
NBR_KR = NBR_G + NB_ROWS
NBR_GROUPS = ROWS_N // NBR_G
NBR_PATTERN_GROUPS = (0, 1, NBR_GROUPS - 1)


def _nbr_window_start(grp):
    return NBR_G * grp - NB_ROWS // 2


def _nbr_bias_body(rpb_ref, o_ref):
    h = pl.program_id(0)
    n_dc = 2 * NB_COLS - 1
    n_dr = 2 * NB_ROWS - 1
    qc = lax.broadcasted_iota(I32, (GRID_W, GRID_W), 0)
    kc = lax.broadcasted_iota(I32, (GRID_W, GRID_W), 1)
    dc = jnp.clip(kc - qc + NB_COLS - 1, 0, n_dc - 1)
    cs = jnp.clip(qc - NB_COLS // 2, 0, GRID_W - NB_COLS)
    col_ok = (kc >= cs) & (kc < cs + NB_COLS)
    tiles = [jnp.zeros((GRID_W, GRID_W), F32) for _ in range(n_dr)]
    for t in range(n_dc):
        hit = dc == t
        for dr in range(n_dr):
            tiles[dr] = jnp.where(hit, rpb_ref[h, dr * n_dc + t] * LOG2E, tiles[dr])
    tiles = [jnp.where(col_ok, x, NEG) for x in tiles]
    masked = jnp.full((GRID_W, GRID_W), NEG, F32)
    for p, grp in enumerate(NBR_PATTERN_GROUPS):
        start = min(max(_nbr_window_start(grp), 0), ROWS_N - NBR_KR)
        for ri in range(NBR_G):
            r = NBR_G * grp + ri
            rs = min(max(r - NB_ROWS // 2, 0), ROWS_N - NB_ROWS)
            row = [tiles[start + kr - r + NB_ROWS - 1] if rs <= start + kr < rs + NB_ROWS else masked
                   for kr in range(NBR_KR)]
            o_ref[0, p, ri * GRID_W:(ri + 1) * GRID_W, :] = jnp.concatenate(row, axis=1)


def nbr_bias_table(rpb):
    heads = rpb.shape[0]
    return pl.pallas_call(
        _nbr_bias_body,
        grid=(heads,),
        in_specs=[pl.BlockSpec(memory_space=pltpu.SMEM)],
        out_specs=pl.BlockSpec((1, 3, NBR_G * GRID_W, NBR_KR * GRID_W), lambda h: (h, 0, 0, 0)),
        out_shape=jax.ShapeDtypeStruct((heads, 3, NBR_G * GRID_W, NBR_KR * GRID_W), F32),
        compiler_params=_cp("parallel"),
        name="nbr_bias_table",
    )(rpb.astype(F32).reshape(heads, -1))


NBR_STEP = 8


def _nbr_attn_body(q_ref, k_ref, v_ref, bias_ref, _, o_ref):
    n_loc = NBR_KR * GRID_W
    rows_g = NBR_G * GRID_W
    for w in range(NBR_STEP):
        grp = pl.program_id(2) * NBR_STEP + w
        pattern = jnp.where(grp == 0, 0, jnp.where(grp == NBR_GROUPS - 1, 2, 1))
        start_row = jnp.clip(_nbr_window_start(grp), 0, ROWS_N - NBR_KR)
        start = pl.multiple_of(start_row * GRID_W, GRID_W)
        rows = slice(w * rows_g, (w + 1) * rows_g)
        q = q_ref[0, rows, :]
        s_loc = _dot_nt(q, k_ref[0, pl.ds(start, n_loc), :]) + bias_ref[0, pattern]
        s_ctx = _dot_nt(q, k_ref[0, S:T, :])
        o = _softmax_pv([(s_loc, v_ref[0, pl.ds(start, n_loc), :]), (s_ctx, v_ref[0, S:T, :])], None, rows_g)
        o_ref[0, rows, :] = o.astype(o_ref.dtype)


def neighbourhood_attention(att, q, k, v, bias_tab, *, heads, col_off, k_col=0, v_col=0):
    tq = NBR_STEP * NBR_G * GRID_W
    return pl.pallas_call(
        _nbr_attn_body,
        grid=(B, heads, NBR_GROUPS // NBR_STEP),
        in_specs=[
            pl.BlockSpec((1, tq, HD), lambda b, h, r: (b, r, h)),
            pl.BlockSpec((1, T, HD), lambda b, h, r: (b, 0, h + k_col)),
            pl.BlockSpec((1, T, HD), lambda b, h, r: (b, 0, h + v_col)),
            pl.BlockSpec((1, 3, NBR_G * GRID_W, NBR_KR * GRID_W), lambda b, h, r: (h, 0, 0, 0)),
            pl.BlockSpec(memory_space=pl.ANY),
        ],
        out_specs=pl.BlockSpec((1, tq, HD), lambda b, h, r: (b, r, h + col_off)),
        out_shape=jax.ShapeDtypeStruct(att.shape, att.dtype),
        input_output_aliases={4: 0},
        compiler_params=_cp("parallel", "parallel", "parallel"),
        name="neighbourhood_attention",
    )(q, k, v, bias_tab, att)


def _cumsum_lanes(mask, tri):
    n = mask.shape[1]
    carry = jnp.zeros((mask.shape[0], 1), F32)
    out = []
    for j in range(n // 256):
        blk = jnp.where(mask[:, j * 256:(j + 1) * 256], 1.0, 0.0).astype(BF16)
        c = _dot(blk, tri) + carry
        carry = c[:, 255:256]
        out.append(c)
    return jnp.concatenate(out, axis=1) if len(out) > 1 else out[0]


def _count(mask):
    return jnp.sum(jnp.where(mask, 1.0, 0.0), axis=1, keepdims=True)


def _topk_slots(affs, caps, tri):
    rows = affs[0].shape[0]

    def step(carry):
        bounds, _ = carry
        new, moved = [], []
        for aff, cap, (lo, hi) in zip(affs, caps, bounds):
            mid = 0.5 * (lo + hi)
            ok = _count(aff >= mid) >= cap
            nlo = jnp.where(ok, mid, lo)
            nhi = jnp.where(ok, hi, mid)
            new.append((nlo, nhi))
            moved.append(jnp.where((nlo != lo) | (nhi != hi), 1.0, 0.0))
        return tuple(new), jnp.max(functools.reduce(jnp.maximum, moved))

    start = tuple((jnp.zeros((rows, 1), F32), jnp.full((rows, 1), 2.0, F32)) for _ in affs)
    bounds, _ = lax.while_loop(lambda c: c[1] > 0.0, step, (start, jnp.float32(1.0)))
    slots = []
    for aff, cap, (lo, hi) in zip(affs, caps, bounds):
        gt = aff >= hi
        eq = (aff >= lo) & (aff < hi)
        need = cap - _count(gt)
        sel = gt | (eq & (_cumsum_lanes(eq, tri) <= need))
        pos = _cumsum_lanes(sel, tri) - 1.0
        slots.append(jnp.where(sel, pos.astype(I32), -1))
    return slots


def _router_body(x_ref, gain_ref, sc_ref, sh_ref, wr_ref, h_ref, slot_ref, aff_ref, lg_scr):
    b = pl.program_id(0)
    tt = pl.program_id(1)
    row = jnp.where(tt == LAT_TILES, CTX_ROW, b)
    y = _rms(x_ref[...], gain_ref[...])
    h = y * (1.0 + sc_ref[0, pl.ds(row, 1), :]) + sh_ref[0, pl.ds(row, 1), :]
    h_ref[...] = h.astype(BF16)
    w = wr_ref[...]
    w_hi, h_hi = w.astype(BF16), h.astype(BF16)
    w_lo = (w - w_hi.astype(F32)).astype(BF16)
    h_lo = (h - h_hi.astype(F32)).astype(BF16)
    lg_scr[tt] = _dot_nt(w_hi, h_hi) + (_dot_nt(w_hi, h_lo) + _dot_nt(w_lo, h_hi))

    @pl.when(tt == TILES - 1)
    def _():
        tri = jnp.where(lax.broadcasted_iota(I32, (256, 256), 0) <= lax.broadcasted_iota(I32, (256, 256), 1),
                        1.0, 0.0).astype(BF16)
        lat = jnp.concatenate([lg_scr[j] for j in range(LAT_TILES)], axis=1)
        affs = []
        for lg in (lat, lg_scr[LAT_TILES]):
            ex = jnp.exp(lg - lg.max(axis=0, keepdims=True))
            affs.append(ex / ex.sum(axis=0, keepdims=True))
        slots = _topk_slots(affs, (CAP_L, CAP_C), tri)
        for (off, n, _, _), aff, slot in zip(_SETS, affs, slots):
            slot_ref[0, :, off:off + n] = slot
            aff_ref[0, :, off:off + n] = aff


def router(x2d, gain, modt, l, w_router_t):
    return pl.pallas_call(
        _router_body,
        grid=(B, TILES),
        in_specs=[
            pl.BlockSpec((TM, D), lambda b, t: (b * TILES + t, 0)),
            pl.BlockSpec((1, D), lambda b, t: (0, 0)),
            pl.BlockSpec((1, 8, D), lambda b, t: (l, 0, 4)),
            pl.BlockSpec((1, 8, D), lambda b, t: (l, 0, 3)),
            pl.BlockSpec((E, D), lambda b, t: (0, 0)),
        ],
        out_specs=[
            pl.BlockSpec((TM, D), lambda b, t: (b * TILES + t, 0)),
            pl.BlockSpec((1, E, T), lambda b, t: (b, 0, 0)),
            pl.BlockSpec((1, E, T), lambda b, t: (b, 0, 0)),
        ],
        out_shape=[
            jax.ShapeDtypeStruct((R, D), BF16),
            jax.ShapeDtypeStruct((B, E, T), I32),
            jax.ShapeDtypeStruct((B, E, T), F32),
        ],
        scratch_shapes=[pltpu.VMEM((TILES, E, TM), F32)],
        compiler_params=_cp("parallel", "arbitrary"),
        name="router",
    )(x2d, gain.reshape(1, D), modt, modt, w_router_t)


_SETS = ((0, S, CAP_L, 0), (S, L, CAP_C, CAP_L))


def _dispatch_body(h_ref, slot_ref, aff_ref, xin_ref, gate_ref):
    e = pl.program_id(1)
    srow = slot_ref[0, pl.ds(e, 1), :]
    arow = aff_ref[0, pl.ds(e, 1), :]
    for off, n, cap, o0 in _SETS:
        hit = lax.broadcasted_iota(I32, (cap, n), 0) == srow[:, off:off + n]
        onehot = jnp.where(hit, 1.0, 0.0).astype(BF16)
        xin_ref[0, 0, o0:o0 + cap, :] = _dot(onehot, h_ref[0, off:off + n, :]).astype(BF16)
        gate_ref[0, 0, o0:o0 + cap, :] = jnp.sum(jnp.where(hit, arow[:, off:off + n], 0.0), axis=1, keepdims=True)


def dispatch(h2, slot, aff):
    return pl.pallas_call(
        _dispatch_body,
        grid=(B, E),
        in_specs=[
            pl.BlockSpec((1, T, D), lambda b, e: (b, 0, 0)),
            pl.BlockSpec((1, E, T), lambda b, e: (b, 0, 0)),
            pl.BlockSpec((1, E, T), lambda b, e: (b, 0, 0)),
        ],
        out_specs=[
            pl.BlockSpec((1, 1, SLOTS, D), lambda b, e: (e, b, 0, 0)),
            pl.BlockSpec((1, 1, SLOTS, 1), lambda b, e: (e, b, 0, 0)),
        ],
        out_shape=[
            jax.ShapeDtypeStruct((E, B, SLOTS, D), BF16),
            jax.ShapeDtypeStruct((E, B, SLOTS, 1), F32),
        ],
        compiler_params=_cp("parallel", "parallel"),
        name="moe_dispatch",
    )(h2.reshape(B, T, D), slot, aff)


def _ffn_body(x_ref, wg_ref, wu_ref, wd_ref, gate_ref, o_ref, acc_ref):
    f = pl.program_id(1)
    wg = wg_ref[0, 0].astype(BF16)
    wu = wu_ref[0, 0].astype(BF16)
    wd = wd_ref[0, 0].astype(BF16)

    @pl.when(f == 0)
    def _():
        acc_ref[...] = jnp.zeros_like(acc_ref)

    for r in range(x_ref.shape[1] // FFN_ROWS):
        rows = slice(r * FFN_ROWS, (r + 1) * FFN_ROWS)
        x = x_ref[0, rows, :]
        a = _dot(x, wg)
        u = _dot(x, wu)
        hid = (a * jax.nn.sigmoid(a) * u).astype(BF16)
        acc_ref[rows, :] += _dot(hid, wd)

    @pl.when(f == pl.num_programs(1) - 1)
    def _():
        o_ref[0] = (acc_ref[...] * gate_ref[0]).astype(o_ref.dtype)


def expert_ffn(xin, gates, w_gate, w_up, w_down, l):
    tf = 256
    m = B * SLOTS
    return pl.pallas_call(
        _ffn_body,
        grid=(E, FE // tf),
        in_specs=[
            pl.BlockSpec((1, m, D), lambda e, f: (e, 0, 0)),
            pl.BlockSpec((1, 1, D, tf), lambda e, f: (l, e, 0, f)),
            pl.BlockSpec((1, 1, D, tf), lambda e, f: (l, e, 0, f)),
            pl.BlockSpec((1, 1, tf, D), lambda e, f: (l, e, f, 0)),
            pl.BlockSpec((1, m, 1), lambda e, f: (e, 0, 0)),
        ],
        out_specs=pl.BlockSpec((1, m, D), lambda e, f: (e, 0, 0)),
        out_shape=jax.ShapeDtypeStruct((E, m, D), BF16),
        scratch_shapes=[pltpu.VMEM((m, D), F32)],
        compiler_params=_cp("parallel", "arbitrary"),
        name="expert_ffn",
    )(xin.reshape(E, m, D), w_gate, w_up, w_down, gates.reshape(E, m, 1))


def _combine_body(o_ref, slot_ref, x_ref, g_ref, out_ref):
    b = pl.program_id(0)
    tt = pl.program_id(2)
    tn = out_ref.shape[1]

    def run(cap, o0, row):
        acc = jnp.zeros((TM, tn), F32)
        sl = lax.broadcasted_iota(I32, (TM, cap), 1)
        for e in range(E):
            onehot = jnp.where(slot_ref[0, :, e:e + 1] == sl, 1.0, 0.0).astype(BF16)
            acc = acc + _dot(onehot, o_ref[e, 0, o0:o0 + cap, :])
        out_ref[...] = x_ref[...] + g_ref[0, pl.ds(row, 1), :] * acc

    @pl.when(tt < LAT_TILES)
    def _():
        run(CAP_L, 0, b)

    @pl.when(tt == LAT_TILES)
    def _():
        run(CAP_C, CAP_L, CTX_ROW)


def combine(o, slot_tok, x2d, modt, l):
    tn = 1024
    per = D // tn
    return pl.pallas_call(
        _combine_body,
        grid=(B, per, TILES),
        in_specs=[
            pl.BlockSpec((E, 1, SLOTS, tn), lambda b, j, t: (0, b, 0, j)),
            pl.BlockSpec((1, TM, E), lambda b, j, t: (b, t, 0)),
            pl.BlockSpec((TM, tn), lambda b, j, t: (b * TILES + t, j)),
            pl.BlockSpec((1, 8, tn), lambda b, j, t: (l, 0, 5 * per + j)),
        ],
        out_specs=pl.BlockSpec((TM, tn), lambda b, j, t: (b * TILES + t, j)),
        out_shape=jax.ShapeDtypeStruct((R, D), F32),
        compiler_params=_cp("parallel", "parallel", "parallel"),
        name="moe_combine",
    )(o.reshape(E, B, SLOTS, D), slot_tok, x2d, modt)


def moe_block(x2d, gain, modt, l, w_router_t, w_gate, w_up, w_down):
    h2, slot, aff = router(x2d, gain, modt, l, w_router_t)
    xin, gates = dispatch(h2, slot, aff)
    o = expert_ffn(xin, gates, w_gate, w_up, w_down, l)
    return combine(o, jnp.swapaxes(slot, 1, 2), x2d, modt, l)


def _rope_tables(d):
    half = d // 2
    t = jnp.arange(S)
    inv = ROPE_BASE ** (-jnp.arange(0, half, 2, dtype=F32) / half)
    parts_c, parts_s = [], []
    for pos in (t // GRID_W, t % GRID_W):
        ang = pos.astype(F32)[:, None] * inv[None, :]
        parts_c += [jnp.cos(ang), jnp.cos(ang)]
        parts_s += [-jnp.sin(ang), jnp.sin(ang)]
    cos = jnp.concatenate([jnp.concatenate(parts_c, axis=1), jnp.ones((L, d), F32)], axis=0)
    sin = jnp.concatenate([jnp.concatenate(parts_s, axis=1), jnp.zeros((L, d), F32)], axis=0)
    return cos, sin


def _rope_lanes(x, cos, sin, quarter):
    lane = lax.broadcasted_iota(I32, x.shape, 1)
    sw = jnp.where(lane % (2 * quarter) < quarter, pltpu.roll(x, HD - quarter, 1), pltpu.roll(x, quarter, 1))
    return x * cos + sw * sin


PROJ_A = 1536


def _ab_attn_epilogue(y, extras, outs):
    (c_ref, s_ref), (aq_ref, ak_ref, av_ref) = extras, outs
    cos, sin = c_ref[...], s_ref[...]
    for h in range(8):
        aq = _rope_lanes(y[:, h * HD:(h + 1) * HD], cos, sin, 32) * QK_SCALE_LOG2
        aq_ref[:, h * HD:(h + 1) * HD] = aq.astype(BF16)
    for h in range(2):
        ak_ref[:, h * HD:(h + 1) * HD] = _rope_lanes(y[:, 1024 + h * HD:1024 + (h + 1) * HD], cos, sin,
                                                     32).astype(BF16)
    av_ref[...] = y[:, 1280:PROJ_A].astype(BF16)


def _ab_mla_epilogue(y, extras, outs):
    (c64p_ref, s64p_ref, c64x2_ref, s64x2_ref, qn_ref, kvn_ref, wq_ref, wkv_ref), (q_ref, k_ref, v_ref) = extras, outs
    cq = _rms(y[:, :MLA_Q], qn_ref[...]).astype(BF16)
    ckv = _rms(y[:, MLA_Q:MLA_Q + MLA_KV], kvn_ref[...]).astype(BF16)
    zeros = jnp.zeros((TM_IN, 64), F32)
    kr = jnp.concatenate([y[:, MLA_Q + MLA_KV:], zeros], axis=1)
    kr = _rope_lanes(kr, c64p_ref[...], s64p_ref[...], 16).astype(BF16)

    yq = _dot(cq, wq_ref[...])
    cos, sin = c64x2_ref[...], s64x2_ref[...]
    for j in range(4):
        rot = _rope_lanes(yq[:, 1024 + j * HD:1024 + (j + 1) * HD], cos, sin, 16)
        for u in range(2):
            h = 2 * j + u
            q_ref[:, 256 * h:256 * h + HD] = (yq[:, h * HD:(h + 1) * HD] * MLA_SCALE_LOG2).astype(BF16)
            q_ref[:, 256 * h + HD:256 * (h + 1)] = jnp.concatenate(
                [rot[:, 64 * u:64 * (u + 1)] * MLA_SCALE_LOG2, zeros], axis=1).astype(BF16)

    ykv = _dot(ckv, wkv_ref[...])
    for h in range(8):
        k_ref[:, 256 * h:256 * h + HD] = ykv[:, h * HD:(h + 1) * HD].astype(BF16)
        k_ref[:, 256 * h + HD:256 * (h + 1)] = kr
    v_ref[...] = ykv[:, 1024:].astype(BF16)


def _cd_attn_epilogue(y, extras, outs):
    (c_ref, s_ref, qn_ref, kn_ref), (cq_ref, ck_ref, cv_ref) = extras, outs
    cos, sin = c_ref[...], s_ref[...]
    for h in range(8):
        x = _rms(y[:, h * HD:(h + 1) * HD], qn_ref[...])
        cq_ref[:, h * HD:(h + 1) * HD] = (_rope_lanes(x, cos, sin, 32) * QK_SCALE_LOG2).astype(BF16)
    for h in range(2):
        x = _rms(y[:, 1024 + h * HD:1024 + (h + 1) * HD], kn_ref[...])
        ck_ref[:, h * HD:(h + 1) * HD] = _rope_lanes(x, cos, sin, 32).astype(BF16)
    cv_ref[...] = y[:, 1280:PROJ_A].astype(BF16)


def _cd_nbr_epilogue(y, extras, outs):
    j = pl.program_id(0)
    outs[0][...] = (y * extras[0][pl.ds(j, 1), :]).astype(BF16)


def _sink_col(sink, kvh, g, tq):
    col = (sink.astype(F32) * LOG2E).reshape(kvh, g, 1, 1)
    return jnp.broadcast_to(col, (kvh, g, tq, 1)).reshape(kvh, g * tq, 1)


def _split_heads(w, n_first):
    k = w.shape[0]
    w3 = w.reshape(k, 8, -1)
    return jnp.concatenate([w3[:, :, :n_first].reshape(k, -1), w3[:, :, n_first:].reshape(k, -1)], axis=1)


def mixer_ab(x2d, gain, modt, l, need_ctx, w_in, sink, q_norm, w_uq, kv_norm, w_ukv, tabs):
    w_in = w_in.astype(BF16)
    table = [(t, True) for t in tabs["r128"]]
    aq, ak, av = project_in(x2d, gain, modt, l, w_in, 0, PROJ_A, 1, _ab_attn_epilogue, table,
                            (1024, 256, 256), "project_in_ab_attn")
    extras = [(t, True) for t in tabs["r64pad"] + tabs["r64x2"]] + [
        (q_norm.reshape(1, MLA_Q), False), (kv_norm.reshape(1, MLA_KV), False),
        (_split_heads(w_uq, 128).astype(BF16), False), (_split_heads(w_ukv, 128).astype(BF16), False)]
    bq, bk, bv = project_in(x2d, gain, modt, l, w_in[:, PROJ_A:], 0, AB_IN - PROJ_A, 1, _ab_mla_epilogue, extras,
                            (2048, 2048, 1024), "project_in_ab_mla")
    aq, ak, av, bq, bk, bv = (t.reshape(B, T, -1) for t in (aq, ak, av, bq, bk, bv))

    ctx_blk = S // L
    att = jnp.zeros((B, T, D), BF16)
    att = window_attention(att, aq, ak, av, _sink_col(sink, 2, 4, WIN_TQ), kvh=2, g=4)
    att = dense_attention(att, bq, bk, bv, kvh=8, g=1, dk=256, dv=HD, tq=S, q_off=0, n_q=1, nk=T,
                          k_blk=0, col_off=8)
    if need_ctx:
        att = dense_attention(att, aq, ak, av, kvh=2, g=4, dk=HD, dv=HD, tq=L, q_off=ctx_blk, n_q=1, nk=L,
                              k_blk=ctx_blk, col_off=0, sink_col=_sink_col(sink, 2, 4, L))
        att = dense_attention(att, bq, bk, bv, kvh=8, g=1, dk=256, dv=HD, tq=L, q_off=ctx_blk, n_q=1, nk=L,
                              k_blk=ctx_blk, col_off=8)
    return att.reshape(R, D)


def mixer_cd(x2d, gain, modt, l, need_ctx, w_in, q_norm, k_norm, rpb, tabs):
    w_in = w_in.astype(BF16)
    extras = [(t, True) for t in tabs["r128"]] + [(q_norm.reshape(1, HD), False), (k_norm.reshape(1, HD), False)]
    cq, ck, cv = (t.reshape(B, T, -1) for t in project_in(
        x2d, gain, modt, l, w_in, 0, PROJ_A, 1, _cd_attn_epilogue, extras, (1024, 256, 256), "project_in_cd_attn"))
    n_d = CD_IN - PROJ_A
    q_scale = jnp.concatenate([jnp.full((1024,), QK_SCALE_LOG2, F32), jnp.ones((n_d - 1024,), F32)])
    dqkv, = project_in(x2d, gain, modt, l, w_in, 1, PROJ_A, n_d // PROJ_A, _cd_nbr_epilogue,
                       [(q_scale.reshape(n_d // PROJ_A, PROJ_A), False)], (n_d,), "project_in_cd_nbr")
    dqkv = dqkv.reshape(B, T, n_d)
    ctx_blk = S // L
    att = jnp.zeros((B, T, D), BF16)
    att = dense_attention(att, cq, ck, cv, kvh=2, g=4, dk=HD, dv=HD, tq=512, q_off=0, n_q=S // 512, nk=T, k_blk=0,
                          col_off=0)
    att = neighbourhood_attention(att, dqkv, dqkv, dqkv, nbr_bias_table(rpb), heads=8, col_off=8, k_col=8,
                                  v_col=16)
    if need_ctx:
        att = dense_attention(att, cq, ck, cv, kvh=2, g=4, dk=HD, dv=HD, tq=L, q_off=ctx_blk, n_q=1, nk=L,
                              k_blk=ctx_blk, col_off=0)
        att = dense_attention(att, dqkv, dqkv, dqkv, kvh=8, g=1, dk=HD, dv=HD, tq=L, q_off=ctx_blk, n_q=1, nk=L,
                              k_blk=ctx_blk, col_off=8, k_col=8, v_col=16)
    return att.reshape(R, D)


def _final_norm_body(x_ref, g_ref, o_ref):
    o_ref[0] = _rms(x_ref[0], g_ref[...])


def final_rmsnorm(x3, gain):
    return pl.pallas_call(
        _final_norm_body,
        grid=(B, S // TM),
        in_specs=[pl.BlockSpec((1, TM, D), lambda b, i: (b, i, 0)), pl.BlockSpec((1, D), lambda b, i: (0, 0))],
        out_specs=pl.BlockSpec((1, TM, D), lambda b, i: (b, i, 0)),
        out_shape=jax.ShapeDtypeStruct((B, S, D), F32),
        compiler_params=_cp("parallel", "parallel"),
        name="final_norm",
    )(x3, gain.reshape(1, D))


def kernel(x, c, ctx, c_ctx, w_ada, b_ada, norm1, norm2, ab_w_in, ab_sink, ab_q_norm, ab_w_uq, ab_kv_norm,
           ab_w_ukv, ab_w_out, cd_w_in, cd_q_norm, cd_k_norm, cd_rpb, cd_w_out, w_router, w_gate, w_up, w_down,
           final_norm):
    c8 = jnp.concatenate([c, c_ctx[None, :], jnp.zeros((8 - B - 1, D), F32)], axis=0)
    modt = ada_modulation(c8, w_ada, b_ada)
    c64, s64 = _rope_tables(64)
    zeros64 = jnp.zeros((T, 64), F32)
    tabs = {
        "r128": _rope_tables(HD),
        "r64x2": (jnp.concatenate([c64, c64], axis=1), jnp.concatenate([s64, s64], axis=1)),
        "r64pad": (jnp.concatenate([c64, zeros64], axis=1), jnp.concatenate([s64, zeros64], axis=1)),
    }
    x2d = jnp.concatenate([x, ctx], axis=1).reshape(R, D)
    for l in range(DEPTH):
        need_ctx = l < DEPTH - 1
        i = l // 2
        if l % 2 == 0:
            att = mixer_ab(x2d, norm1[l], modt, l, need_ctx, ab_w_in[i], ab_sink[i], ab_q_norm[i], ab_w_uq[i],
                           ab_kv_norm[i], ab_w_ukv[i], tabs)
            w_out = ab_w_out[i]
        else:
            att = mixer_cd(x2d, norm1[l], modt, l, need_ctx, cd_w_in[i], cd_q_norm[i], cd_k_norm[i], cd_rpb[i],
                           tabs)
            w_out = cd_w_out[i]
        x2d = matmul_gated_residual(att, w_out.astype(BF16), x2d, modt, l, 2)
        x2d = moe_block(x2d, norm2[l], modt, l, w_router[l].T, w_gate, w_up, w_down)
    return final_rmsnorm(x2d.reshape(B, T, D), final_norm)
```

```python
import functools

import jax
import jax.numpy as jnp
from jax import lax
from jax.experimental import pallas as pl
from jax.experimental.pallas import tpu as pltpu

F32 = jnp.float32
BF16 = jnp.bfloat16
I32 = jnp.int32

D = 2048
B = 4
S = 2048
L = 256
T = S + L
R = B * T
DEPTH = 4
GRID_W = 64
HD = 128
ROPE_BASE = 10000.0
EPS = 1e-6
NEG = -1e30
LOG2E = 1.4426950408889634
QK_SCALE_LOG2 = HD ** -0.5 * LOG2E
MLA_SCALE_LOG2 = 192 ** -0.5 * LOG2E
WINDOW = 128
BQ = 128
NB_ROWS = 8
NB_COLS = 16
E = 16
CAP_L = 2 * S // E
CAP_C = 2 * L // E
SLOTS = CAP_L + CAP_C
FE = 1024
FFN_ROWS = 576
MLA_Q = 512
MLA_KV = 256
AB_IN = 2368
CD_IN = 4608

ATTN_SUB = 256
TM = 256
TILES = T // TM
LAT_TILES = S // TM
CTX_ROW = B
VMEM_LIMIT = 56 * 1024 * 1024


def _cp(*sem):
    return pltpu.CompilerParams(dimension_semantics=sem, vmem_limit_bytes=VMEM_LIMIT)


def _dot(a, b):
    return jnp.dot(a, b, preferred_element_type=F32)


def _dot_nt(a, b):
    return lax.dot_general(a, b, (((1,), (1,)), ((), ())), preferred_element_type=F32)


def _ada_body(c_ref, w_ref, b_ref, o_ref):
    c = c_ref[...]
    a = c * jax.nn.sigmoid(c)
    o_ref[0] = _dot(a.astype(BF16), w_ref[0].astype(BF16)) + b_ref[0]


def ada_modulation(c8, w_ada, b_ada):
    tn = 1024
    n = 6 * D
    return pl.pallas_call(
        _ada_body,
        grid=(DEPTH, n // tn),
        in_specs=[
            pl.BlockSpec((8, D), lambda l, j: (0, 0)),
            pl.BlockSpec((1, D, tn), lambda l, j: (l, 0, j)),
            pl.BlockSpec((1, 1, tn), lambda l, j: (l, 0, j)),
        ],
        out_specs=pl.BlockSpec((1, 8, tn), lambda l, j: (l, 0, j)),
        out_shape=jax.ShapeDtypeStruct((DEPTH, 8, n), F32),
        compiler_params=_cp("parallel", "parallel"),
        name="ada_modulation",
    )(c8, w_ada, b_ada.reshape(DEPTH, 1, n))


def _rms(x, gain):
    return x * lax.rsqrt(jnp.mean(x * x, axis=-1, keepdims=True) + EPS) * gain


TM_IN = T // 6
TM_BIG = T // 3


def _row_mod(g_ref, tile, tm):
    per = T // tm
    r = (tile % per) * tm + lax.broadcasted_iota(I32, (tm, 1), 0)
    return jnp.where(r >= S, g_ref[0, CTX_ROW:CTX_ROW + 1, :], g_ref[0, pl.ds(tile // per, 1), :])


def _proj_in_body(*refs, n_extra, epilogue):
    x_ref, gain_ref, sc_ref, sh_ref, w_ref = refs[:5]
    tile = pl.program_id(1)
    h = _rms(x_ref[...], gain_ref[...]) * (1.0 + _row_mod(sc_ref, tile, TM_IN)) + _row_mod(sh_ref, tile, TM_IN)
    epilogue(_dot(h.astype(BF16), w_ref[...]), refs[5:5 + n_extra], refs[5 + n_extra:])


def project_in(x2d, gain, modt, l, w, col_blk, tn, n_tiles, epilogue, extras, out_widths, name):
    tabs_per_batch = T // TM_IN
    extra_specs = [pl.BlockSpec((TM_IN, HD), lambda j, i: (i % tabs_per_batch, 0)) if is_table
                   else pl.BlockSpec(a.shape, lambda j, i: (0, 0)) for a, is_table in extras]
    if n_tiles > 1:
        out_specs = [pl.BlockSpec((TM_IN, tn), lambda j, i: (i, j))]
    else:
        out_specs = [pl.BlockSpec((TM_IN, wd), lambda j, i: (i, 0)) for wd in out_widths]
    return pl.pallas_call(
        functools.partial(_proj_in_body, n_extra=len(extras), epilogue=epilogue),
        grid=(n_tiles, R // TM_IN),
        in_specs=[
            pl.BlockSpec((TM_IN, D), lambda j, i: (i, 0)),
            pl.BlockSpec((1, D), lambda j, i: (0, 0)),
            pl.BlockSpec((1, 8, D), lambda j, i: (l, 0, 1)),
            pl.BlockSpec((1, 8, D), lambda j, i: (l, 0, 0)),
            pl.BlockSpec((D, tn), lambda j, i: (0, col_blk + j)),
        ] + extra_specs,
        out_specs=out_specs,
        out_shape=[jax.ShapeDtypeStruct((R, wd), BF16) for wd in out_widths],
        compiler_params=_cp("parallel", "parallel"),
        name=name,
    )(x2d, gain.reshape(1, D), modt, modt, w, *[a for a, _ in extras])


def _mm_res_body(a_ref, b_ref, x_ref, g_ref, o_ref):
    y = _dot(a_ref[...], b_ref[...])
    o_ref[...] = x_ref[...] + _row_mod(g_ref, pl.program_id(1), TM_BIG) * y


def matmul_gated_residual(a, b, x2d, modt, l, chunk):
    tn = 1024
    k = a.shape[1]
    per = D // tn
    return pl.pallas_call(
        _mm_res_body,
        grid=(D // tn, R // TM_BIG),
        in_specs=[
            pl.BlockSpec((TM_BIG, k), lambda j, i: (i, 0)),
            pl.BlockSpec((k, tn), lambda j, i: (0, j)),
            pl.BlockSpec((TM_BIG, tn), lambda j, i: (i, j)),
            pl.BlockSpec((1, 8, tn), lambda j, i: (l, 0, chunk * per + j)),
        ],
        out_specs=pl.BlockSpec((TM_BIG, tn), lambda j, i: (i, j)),
        out_shape=jax.ShapeDtypeStruct((R, D), F32),
        compiler_params=_cp("parallel", "parallel"),
        name="matmul_gated_residual",
    )(a, b, x2d, modt)


def _softmax_pv(parts, sink, rows):
    m = parts[0][0].max(axis=-1, keepdims=True)
    for s, _ in parts[1:]:
        m = jnp.maximum(m, s.max(axis=-1, keepdims=True))
    if sink is not None:
        m = jnp.maximum(m, sink)
    den = jnp.zeros((rows, 1), F32) if sink is None else jnp.exp2(sink - m)
    acc = None
    for s, v in parts:
        p = jnp.exp2(s - m)
        den = den + p.sum(axis=-1, keepdims=True)
        pv = _dot(p.astype(BF16), v)
        acc = pv if acc is None else acc + pv
    return acc / den


def _dense_attn_body(*refs, g, tq, dk, dv, has_sink):
    if has_sink:
        q_ref, k_ref, v_ref, sink_ref, _, o_ref = refs
        sink = sink_ref[0]
    else:
        q_ref, k_ref, v_ref, _, o_ref = refs
        sink = None
    sub = min(tq, ATTN_SUB)
    for j in range(g):
        for r in range(tq // sub):
            rows = slice(r * sub, (r + 1) * sub)
            s = _dot_nt(q_ref[0, rows, j * dk:(j + 1) * dk], k_ref[0])
            sk = None if sink is None else sink[j * tq + r * sub:j * tq + (r + 1) * sub]
            o = _softmax_pv([(s, v_ref[0])], sk, sub)
            o_ref[0, rows, j * dv:(j + 1) * dv] = o.astype(o_ref.dtype)


def dense_attention(att, q, k, v, *, kvh, g, dk, dv, tq, q_off, n_q, nk, k_blk, col_off, sink_col=None,
                    k_col=0, v_col=0):
    in_specs = [
        pl.BlockSpec((1, tq, g * dk), lambda b, h, i: (b, i + q_off, h)),
        pl.BlockSpec((1, nk, dk), lambda b, h, i: (b, k_blk, h + k_col)),
        pl.BlockSpec((1, nk, dv), lambda b, h, i: (b, k_blk, h + v_col)),
    ]
    args = [q, k, v]
    if sink_col is not None:
        in_specs.append(pl.BlockSpec((1, g * tq, 1), lambda b, h, i: (h, 0, 0)))
        args.append(sink_col)
    in_specs.append(pl.BlockSpec(memory_space=pl.ANY))
    args.append(att)
    return pl.pallas_call(
        functools.partial(_dense_attn_body, g=g, tq=tq, dk=dk, dv=dv, has_sink=sink_col is not None),
        grid=(B, kvh, n_q),
        in_specs=in_specs,
        out_specs=pl.BlockSpec((1, tq, g * dv), lambda b, h, i: (b, i + q_off, h + col_off)),
        out_shape=jax.ShapeDtypeStruct(att.shape, att.dtype),
        input_output_aliases={len(args) - 1: 0},
        compiler_params=_cp("parallel", "parallel", "parallel"),
        name="dense_attention",
    )(*args)


WIN_TQ = 512
WIN_SPAN = WIN_TQ + 2 * WINDOW


def _window_attn_body(q_ref, k_ref, v_ref, sink_ref, _, o_ref, *, g):
    q0 = pl.program_id(2) * WIN_TQ
    start = pl.multiple_of(jnp.clip(q0 - WINDOW, 0, S - WIN_SPAN), WINDOW)
    kl = k_ref[0, pl.ds(start, WIN_SPAN), :]
    vl = v_ref[0, pl.ds(start, WIN_SPAN), :]
    qi = lax.broadcasted_iota(I32, (WIN_TQ, WIN_SPAN), 0) + q0
    kj = lax.broadcasted_iota(I32, (WIN_TQ, WIN_SPAN), 1) + start
    valid = jnp.abs(qi - kj) <= WINDOW
    for j in range(g):
        q = q_ref[0, :, j * HD:(j + 1) * HD]
        s_loc = jnp.where(valid, _dot_nt(q, kl), NEG)
        s_ctx = _dot_nt(q, k_ref[0, S:T, :])
        o = _softmax_pv([(s_loc, vl), (s_ctx, v_ref[0, S:T, :])], sink_ref[0, j * WIN_TQ:(j + 1) * WIN_TQ],
                        WIN_TQ)
        o_ref[0, :, j * HD:(j + 1) * HD] = o.astype(o_ref.dtype)


def window_attention(att, q, k, v, sink_col, *, kvh, g):
    tq = WIN_TQ
    return pl.pallas_call(
        functools.partial(_window_attn_body, g=g),
        grid=(B, kvh, S // tq),
        in_specs=[
            pl.BlockSpec((1, tq, g * HD), lambda b, h, m: (b, m, h)),
            pl.BlockSpec((1, T, HD), lambda b, h, m: (b, 0, h)),
            pl.BlockSpec((1, T, HD), lambda b, h, m: (b, 0, h)),
            pl.BlockSpec((1, g * WIN_TQ, 1), lambda b, h, m: (h, 0, 0)),
            pl.BlockSpec(memory_space=pl.ANY),
        ],
        out_specs=pl.BlockSpec((1, tq, g * HD), lambda b, h, m: (b, m, h)),
        out_shape=jax.ShapeDtypeStruct(att.shape, att.dtype),
        input_output_aliases={4: 0},
        compiler_params=_cp("parallel", "parallel", "parallel"),
        name="window_attention",
    )(q, k, v, sink_col, att)


ROWS_N = S // GRID_W
NBR_G = 4
NBR_KR = NBR_G + NB_ROWS
NBR_GROUPS = ROWS_N // NBR_G
NBR_PATTERN_GROUPS = (0, 1, NBR_GROUPS - 1)


def _nbr_window_start(grp):
    return NBR_G * grp - NB_ROWS // 2


def _nbr_bias_body(rpb_ref, o_ref):
    h = pl.program_id(0)
    n_dc = 2 * NB_COLS - 1
    n_dr = 2 * NB_ROWS - 1
    qc = lax.broadcasted_iota(I32, (GRID_W, GRID_W), 0)
    kc = lax.broadcasted_iota(I32, (GRID_W, GRID_W), 1)
    dc = jnp.clip(kc - qc + NB_COLS - 1, 0, n_dc - 1)
    cs = jnp.clip(qc - NB_COLS // 2, 0, GRID_W - NB_COLS)
    col_ok = (kc >= cs) & (kc < cs + NB_COLS)
    tiles = [jnp.zeros((GRID_W, GRID_W), F32) for _ in range(n_dr)]
    for t in range(n_dc):
        hit = dc == t
        for dr in range(n_dr):
            tiles[dr] = jnp.where(hit, rpb_ref[h, dr * n_dc + t] * LOG2E, tiles[dr])
    tiles = [jnp.where(col_ok, x, NEG) for x in tiles]
    masked = jnp.full((GRID_W, GRID_W), NEG, F32)
    for p, grp in enumerate(NBR_PATTERN_GROUPS):
        start = min(max(_nbr_window_start(grp), 0), ROWS_N - NBR_KR)
        for ri in range(NBR_G):
            r = NBR_G * grp + ri
            rs = min(max(r - NB_ROWS // 2, 0), ROWS_N - NB_ROWS)
            row = [tiles[start + kr - r + NB_ROWS - 1] if rs <= start + kr < rs + NB_ROWS else masked
                   for kr in range(NBR_KR)]
            o_ref[0, p, ri * GRID_W:(ri + 1) * GRID_W, :] = jnp.concatenate(row, axis=1)


def nbr_bias_table(rpb):
    heads = rpb.shape[0]
    return pl.pallas_call(
        _nbr_bias_body,
        grid=(heads,),
        in_specs=[pl.BlockSpec(memory_space=pltpu.SMEM)],
        out_specs=pl.BlockSpec((1, 3, NBR_G * GRID_W, NBR_KR * GRID_W), lambda h: (h, 0, 0, 0)),
        out_shape=jax.ShapeDtypeStruct((heads, 3, NBR_G * GRID_W, NBR_KR * GRID_W), F32),
        compiler_params=_cp("parallel"),
        name="nbr_bias_table",
    )(rpb.astype(F32).reshape(heads, -1))


NBR_STEP = 8


def _nbr_attn_body(q_ref, k_ref, v_ref, bias_ref, _, o_ref):
    n_loc = NBR_KR * GRID_W
    rows_g = NBR_G * GRID_W
    for w in range(NBR_STEP):
        grp = pl.program_id(2) * NBR_STEP + w
        pattern = jnp.where(grp == 0, 0, jnp.where(grp == NBR_GROUPS - 1, 2, 1))
        start_row = jnp.clip(_nbr_window_start(grp), 0, ROWS_N - NBR_KR)
        start = pl.multiple_of(start_row * GRID_W, GRID_W)
        rows = slice(w * rows_g, (w + 1) * rows_g)
        q = q_ref[0, rows, :]
        s_loc = _dot_nt(q, k_ref[0, pl.ds(start, n_loc), :]) + bias_ref[0, pattern]
        s_ctx = _dot_nt(q, k_ref[0, S:T, :])
        o = _softmax_pv([(s_loc, v_ref[0, pl.ds(start, n_loc), :]), (s_ctx, v_ref[0, S:T, :])], None, rows_g)
        o_ref[0, rows, :] = o.astype(o_ref.dtype)


def neighbourhood_attention(att, q, k, v, bias_tab, *, heads, col_off, k_col=0, v_col=0):
    tq = NBR_STEP * NBR_G * GRID_W
    return pl.pallas_call(
        _nbr_attn_body,
        grid=(B, heads, NBR_GROUPS // NBR_STEP),
        in_specs=[
            pl.BlockSpec((1, tq, HD), lambda b, h, r: (b, r, h)),
            pl.BlockSpec((1, T, HD), lambda b, h, r: (b, 0, h + k_col)),
            pl.BlockSpec((1, T, HD), lambda b, h, r: (b, 0, h + v_col)),
            pl.BlockSpec((1, 3, NBR_G * GRID_W, NBR_KR * GRID_W), lambda b, h, r: (h, 0, 0, 0)),
            pl.BlockSpec(memory_space=pl.ANY),
        ],
        out_specs=pl.BlockSpec((1, tq, HD), lambda b, h, r: (b, r, h + col_off)),
        out_shape=jax.ShapeDtypeStruct(att.shape, att.dtype),
        input_output_aliases={4: 0},
        compiler_params=_cp("parallel", "parallel", "parallel"),
        name="neighbourhood_attention",
    )(q, k, v, bias_tab, att)


def _cumsum_lanes(mask, tri):
    n = mask.shape[1]
    carry = jnp.zeros((mask.shape[0], 1), F32)
    out = []
    for j in range(n // 256):
        blk = jnp.where(mask[:, j * 256:(j + 1) * 256], 1.0, 0.0).astype(BF16)
        c = _dot(blk, tri) + carry
        carry = c[:, 255:256]
        out.append(c)
    return jnp.concatenate(out, axis=1) if len(out) > 1 else out[0]


def _count(mask):
    return jnp.sum(jnp.where(mask, 1.0, 0.0), axis=1, keepdims=True)


def _topk_slots(affs, caps, tri):
    rows = affs[0].shape[0]

    def step(carry):
        bounds, _ = carry
        new, moved = [], []
        for aff, cap, (lo, hi) in zip(affs, caps, bounds):
            mid = 0.5 * (lo + hi)
            ok = _count(aff >= mid) >= cap
            nlo = jnp.where(ok, mid, lo)
            nhi = jnp.where(ok, hi, mid)
            new.append((nlo, nhi))
            moved.append(jnp.where((nlo != lo) | (nhi != hi), 1.0, 0.0))
        return tuple(new), jnp.max(functools.reduce(jnp.maximum, moved))

    start = tuple((jnp.zeros((rows, 1), F32), jnp.full((rows, 1), 2.0, F32)) for _ in affs)
    bounds, _ = lax.while_loop(lambda c: c[1] > 0.0, step, (start, jnp.float32(1.0)))
    slots = []
    for aff, cap, (lo, hi) in zip(affs, caps, bounds):
        gt = aff >= hi
        eq = (aff >= lo) & (aff < hi)
        need = cap - _count(gt)
        sel = gt | (eq & (_cumsum_lanes(eq, tri) <= need))
        pos = _cumsum_lanes(sel, tri) - 1.0
        slots.append(jnp.where(sel, pos.astype(I32), -1))
    return slots


def _router_body(x_ref, gain_ref, sc_ref, sh_ref, wr_ref, h_ref, slot_ref, aff_ref, lg_scr):
    b = pl.program_id(0)
    tt = pl.program_id(1)
    row = jnp.where(tt == LAT_TILES, CTX_ROW, b)
    y = _rms(x_ref[...], gain_ref[...])
    h = y * (1.0 + sc_ref[0, pl.ds(row, 1), :]) + sh_ref[0, pl.ds(row, 1), :]
    h_ref[...] = h.astype(BF16)
    w = wr_ref[...]
    w_hi, h_hi = w.astype(BF16), h.astype(BF16)
    w_lo = (w - w_hi.astype(F32)).astype(BF16)
    h_lo = (h - h_hi.astype(F32)).astype(BF16)
    lg_scr[tt] = _dot_nt(w_hi, h_hi) + (_dot_nt(w_hi, h_lo) + _dot_nt(w_lo, h_hi))

    @pl.when(tt == TILES - 1)
    def _():
        tri = jnp.where(lax.broadcasted_iota(I32, (256, 256), 0) <= lax.broadcasted_iota(I32, (256, 256), 1),
                        1.0, 0.0).astype(BF16)
        lat = jnp.concatenate([lg_scr[j] for j in range(LAT_TILES)], axis=1)
        affs = []
        for lg in (lat, lg_scr[LAT_TILES]):
            ex = jnp.exp(lg - lg.max(axis=0, keepdims=True))
            affs.append(ex / ex.sum(axis=0, keepdims=True))
        slots = _topk_slots(affs, (CAP_L, CAP_C), tri)
        for (off, n, _, _), aff, slot in zip(_SETS, affs, slots):
            slot_ref[0, :, off:off + n] = slot
            aff_ref[0, :, off:off + n] = aff


def router(x2d, gain, modt, l, w_router_t):
    return pl.pallas_call(
        _router_body,
        grid=(B, TILES),
        in_specs=[
            pl.BlockSpec((TM, D), lambda b, t: (b * TILES + t, 0)),
            pl.BlockSpec((1, D), lambda b, t: (0, 0)),
            pl.BlockSpec((1, 8, D), lambda b, t: (l, 0, 4)),
            pl.BlockSpec((1, 8, D), lambda b, t: (l, 0, 3)),
            pl.BlockSpec((E, D), lambda b, t: (0, 0)),
        ],
        out_specs=[
            pl.BlockSpec((TM, D), lambda b, t: (b * TILES + t, 0)),
            pl.BlockSpec((1, E, T), lambda b, t: (b, 0, 0)),
            pl.BlockSpec((1, E, T), lambda b, t: (b, 0, 0)),
        ],
        out_shape=[
            jax.ShapeDtypeStruct((R, D), BF16),
            jax.ShapeDtypeStruct((B, E, T), I32),
            jax.ShapeDtypeStruct((B, E, T), F32),
        ],
        scratch_shapes=[pltpu.VMEM((TILES, E, TM), F32)],
        compiler_params=_cp("parallel", "arbitrary"),
        name="router",
    )(x2d, gain.reshape(1, D), modt, modt, w_router_t)


_SETS = ((0, S, CAP_L, 0), (S, L, CAP_C, CAP_L))


def _dispatch_body(h_ref, slot_ref, aff_ref, xin_ref, gate_ref):
    e = pl.program_id(1)
    srow = slot_ref[0, pl.ds(e, 1), :]
    arow = aff_ref[0, pl.ds(e, 1), :]
    for off, n, cap, o0 in _SETS:
        hit = lax.broadcasted_iota(I32, (cap, n), 0) == srow[:, off:off + n]
        onehot = jnp.where(hit, 1.0, 0.0).astype(BF16)
        xin_ref[0, 0, o0:o0 + cap, :] = _dot(onehot, h_ref[0, off:off + n, :]).astype(BF16)
        gate_ref[0, 0, o0:o0 + cap, :] = jnp.sum(jnp.where(hit, arow[:, off:off + n], 0.0), axis=1, keepdims=True)


def dispatch(h2, slot, aff):
    return pl.pallas_call(
        _dispatch_body,
        grid=(B, E),
        in_specs=[
            pl.BlockSpec((1, T, D), lambda b, e: (b, 0, 0)),
            pl.BlockSpec((1, E, T), lambda b, e: (b, 0, 0)),
            pl.BlockSpec((1, E, T), lambda b, e: (b, 0, 0)),
        ],
        out_specs=[
            pl.BlockSpec((1, 1, SLOTS, D), lambda b, e: (e, b, 0, 0)),
            pl.BlockSpec((1, 1, SLOTS, 1), lambda b, e: (e, b, 0, 0)),
        ],
        out_shape=[
            jax.ShapeDtypeStruct((E, B, SLOTS, D), BF16),
            jax.ShapeDtypeStruct((E, B, SLOTS, 1), F32),
        ],
        compiler_params=_cp("parallel", "parallel"),
        name="moe_dispatch",
    )(h2.reshape(B, T, D), slot, aff)


def _ffn_body(x_ref, wg_ref, wu_ref, wd_ref, gate_ref, o_ref, acc_ref):
    f = pl.program_id(1)
    wg = wg_ref[0, 0].astype(BF16)
    wu = wu_ref[0, 0].astype(BF16)
    wd = wd_ref[0, 0].astype(BF16)

    @pl.when(f == 0)
    def _():
        acc_ref[...] = jnp.zeros_like(acc_ref)

    for r in range(x_ref.shape[1] // FFN_ROWS):
        rows = slice(r * FFN_ROWS, (r + 1) * FFN_ROWS)
        x = x_ref[0, rows, :]
        a = _dot(x, wg)
        u = _dot(x, wu)
        hid = (a * jax.nn.sigmoid(a) * u).astype(BF16)
        acc_ref[rows, :] += _dot(hid, wd)

    @pl.when(f == pl.num_programs(1) - 1)
    def _():
        o_ref[0] = (acc_ref[...] * gate_ref[0]).astype(o_ref.dtype)


def expert_ffn(xin, gates, w_gate, w_up, w_down, l):
    tf = 256
    m = B * SLOTS
    return pl.pallas_call(
        _ffn_body,
        grid=(E, FE // tf),
        in_specs=[
            pl.BlockSpec((1, m, D), lambda e, f: (e, 0, 0)),
            pl.BlockSpec((1, 1, D, tf), lambda e, f: (l, e, 0, f)),
            pl.BlockSpec((1, 1, D, tf), lambda e, f: (l, e, 0, f)),
            pl.BlockSpec((1, 1, tf, D), lambda e, f: (l, e, f, 0)),
            pl.BlockSpec((1, m, 1), lambda e, f: (e, 0, 0)),
        ],
        out_specs=pl.BlockSpec((1, m, D), lambda e, f: (e, 0, 0)),
        out_shape=jax.ShapeDtypeStruct((E, m, D), BF16),
        scratch_shapes=[pltpu.VMEM((m, D), F32)],
        compiler_params=_cp("parallel", "arbitrary"),
        name="expert_ffn",
    )(xin.reshape(E, m, D), w_gate, w_up, w_down, gates.reshape(E, m, 1))


def _combine_body(o_ref, slot_ref, x_ref, g_ref, out_ref):
    b = pl.program_id(0)
    tt = pl.program_id(2)
    tn = out_ref.shape[1]

    def run(cap, o0, row):
        acc = jnp.zeros((TM, tn), F32)
        sl = lax.broadcasted_iota(I32, (TM, cap), 1)
        for e in range(E):
            onehot = jnp.where(slot_ref[0, :, e:e + 1] == sl, 1.0, 0.0).astype(BF16)
            acc = acc + _dot(onehot, o_ref[e, 0, o0:o0 + cap, :])
        out_ref[...] = x_ref[...] + g_ref[0, pl.ds(row, 1), :] * acc

    @pl.when(tt < LAT_TILES)
    def _():
        run(CAP_L, 0, b)

    @pl.when(tt == LAT_TILES)
    def _():
        run(CAP_C, CAP_L, CTX_ROW)


def combine(o, slot_tok, x2d, modt, l):
    tn = 1024
    per = D // tn
    return pl.pallas_call(
        _combine_body,
        grid=(B, per, TILES),
        in_specs=[
            pl.BlockSpec((E, 1, SLOTS, tn), lambda b, j, t: (0, b, 0, j)),
            pl.BlockSpec((1, TM, E), lambda b, j, t: (b, t, 0)),
            pl.BlockSpec((TM, tn), lambda b, j, t: (b * TILES + t, j)),
            pl.BlockSpec((1, 8, tn), lambda b, j, t: (l, 0, 5 * per + j)),
        ],
        out_specs=pl.BlockSpec((TM, tn), lambda b, j, t: (b * TILES + t, j)),
        out_shape=jax.ShapeDtypeStruct((R, D), F32),
        compiler_params=_cp("parallel", "parallel", "parallel"),
        name="moe_combine",
    )(o.reshape(E, B, SLOTS, D), slot_tok, x2d, modt)


def moe_block(x2d, gain, modt, l, w_router_t, w_gate, w_up, w_down):
    h2, slot, aff = router(x2d, gain, modt, l, w_router_t)
    xin, gates = dispatch(h2, slot, aff)
    o = expert_ffn(xin, gates, w_gate, w_up, w_down, l)
    return combine(o, jnp.swapaxes(slot, 1, 2), x2d, modt, l)


def _rope_tables(d):
    half = d // 2
    t = jnp.arange(S)
    inv = ROPE_BASE ** (-jnp.arange(0, half, 2, dtype=F32) / half)
    parts_c, parts_s = [], []
    for pos in (t // GRID_W, t % GRID_W):
        ang = pos.astype(F32)[:, None] * inv[None, :]
        parts_c += [jnp.cos(ang), jnp.cos(ang)]
        parts_s += [-jnp.sin(ang), jnp.sin(ang)]
    cos = jnp.concatenate([jnp.concatenate(parts_c, axis=1), jnp.ones((L, d), F32)], axis=0)
    sin = jnp.concatenate([jnp.concatenate(parts_s, axis=1), jnp.zeros((L, d), F32)], axis=0)
    return cos, sin


def _rope_lanes(x, cos, sin, quarter):
    lane = lax.broadcasted_iota(I32, x.shape, 1)
    sw = jnp.where(lane % (2 * quarter) < quarter, pltpu.roll(x, HD - quarter, 1), pltpu.roll(x, quarter, 1))
    return x * cos + sw * sin


PROJ_A = 1536


def _ab_attn_epilogue(y, extras, outs):
    (c_ref, s_ref), (aq_ref, ak_ref, av_ref) = extras, outs
    cos, sin = c_ref[...], s_ref[...]
    for h in range(8):
        aq = _rope_lanes(y[:, h * HD:(h + 1) * HD], cos, sin, 32) * QK_SCALE_LOG2
        aq_ref[:, h * HD:(h + 1) * HD] = aq.astype(BF16)
    for h in range(2):
        ak_ref[:, h * HD:(h + 1) * HD] = _rope_lanes(y[:, 1024 + h * HD:1024 + (h + 1) * HD], cos, sin,
                                                     32).astype(BF16)
    av_ref[...] = y[:, 1280:PROJ_A].astype(BF16)


def _ab_mla_epilogue(y, extras, outs):
    (c64p_ref, s64p_ref, c64x2_ref, s64x2_ref, qn_ref, kvn_ref, wq_ref, wkv_ref), (q_ref, k_ref, v_ref) = extras, outs
    cq = _rms(y[:, :MLA_Q], qn_ref[...]).astype(BF16)
    ckv = _rms(y[:, MLA_Q:MLA_Q + MLA_KV], kvn_ref[...]).astype(BF16)
    zeros = jnp.zeros((TM_IN, 64), F32)
    kr = jnp.concatenate([y[:, MLA_Q + MLA_KV:], zeros], axis=1)
    kr = _rope_lanes(kr, c64p_ref[...], s64p_ref[...], 16).astype(BF16)

    yq = _dot(cq, wq_ref[...])
    cos, sin = c64x2_ref[...], s64x2_ref[...]
    for j in range(4):
        rot = _rope_lanes(yq[:, 1024 + j * HD:1024 + (j + 1) * HD], cos, sin, 16)
        for u in range(2):
            h = 2 * j + u
            q_ref[:, 256 * h:256 * h + HD] = (yq[:, h * HD:(h + 1) * HD] * MLA_SCALE_LOG2).astype(BF16)
            q_ref[:, 256 * h + HD:256 * (h + 1)] = jnp.concatenate(
                [rot[:, 64 * u:64 * (u + 1)] * MLA_SCALE_LOG2, zeros], axis=1).astype(BF16)

    ykv = _dot(ckv, wkv_ref[...])
    for h in range(8):
        k_ref[:, 256 * h:256 * h + HD] = ykv[:, h * HD:(h + 1) * HD].astype(BF16)
        k_ref[:, 256 * h + HD:256 * (h + 1)] = kr
    v_ref[...] = ykv[:, 1024:].astype(BF16)


def _cd_attn_epilogue(y, extras, outs):
    (c_ref, s_ref, qn_ref, kn_ref), (cq_ref, ck_ref, cv_ref) = extras, outs
    cos, sin = c_ref[...], s_ref[...]
    for h in range(8):
        x = _rms(y[:, h * HD:(h + 1) * HD], qn_ref[...])
        cq_ref[:, h * HD:(h + 1) * HD] = (_rope_lanes(x, cos, sin, 32) * QK_SCALE_LOG2).astype(BF16)
    for h in range(2):
        x = _rms(y[:, 1024 + h * HD:1024 + (h + 1) * HD], kn_ref[...])
        ck_ref[:, h * HD:(h + 1) * HD] = _rope_lanes(x, cos, sin, 32).astype(BF16)
    cv_ref[...] = y[:, 1280:PROJ_A].astype(BF16)


def _cd_nbr_epilogue(y, extras, outs):
    j = pl.program_id(0)
    outs[0][...] = (y * extras[0][pl.ds(j, 1), :]).astype(BF16)


def _sink_col(sink, kvh, g, tq):
    col = (sink.astype(F32) * LOG2E).reshape(kvh, g, 1, 1)
    return jnp.broadcast_to(col, (kvh, g, tq, 1)).reshape(kvh, g * tq, 1)


def _split_heads(w, n_first):
    k = w.shape[0]
    w3 = w.reshape(k, 8, -1)
    return jnp.concatenate([w3[:, :, :n_first].reshape(k, -1), w3[:, :, n_first:].reshape(k, -1)], axis=1)


def mixer_ab(x2d, gain, modt, l, need_ctx, w_in, sink, q_norm, w_uq, kv_norm, w_ukv, tabs):
    w_in = w_in.astype(BF16)
    table = [(t, True) for t in tabs["r128"]]
    aq, ak, av = project_in(x2d, gain, modt, l, w_in, 0, PROJ_A, 1, _ab_attn_epilogue, table,
                            (1024, 256, 256), "project_in_ab_attn")
    extras = [(t, True) for t in tabs["r64pad"] + tabs["r64x2"]] + [
        (q_norm.reshape(1, MLA_Q), False), (kv_norm.reshape(1, MLA_KV), False),
        (_split_heads(w_uq, 128).astype(BF16), False), (_split_heads(w_ukv, 128).astype(BF16), False)]
    bq, bk, bv = project_in(x2d, gain, modt, l, w_in[:, PROJ_A:], 0, AB_IN - PROJ_A, 1, _ab_mla_epilogue, extras,
                            (2048, 2048, 1024), "project_in_ab_mla")
    aq, ak, av, bq, bk, bv = (t.reshape(B, T, -1) for t in (aq, ak, av, bq, bk, bv))

    ctx_blk = S // L
    att = jnp.zeros((B, T, D), BF16)
    att = window_attention(att, aq, ak, av, _sink_col(sink, 2, 4, WIN_TQ), kvh=2, g=4)
    att = dense_attention(att, bq, bk, bv, kvh=8, g=1, dk=256, dv=HD, tq=S, q_off=0, n_q=1, nk=T,
                          k_blk=0, col_off=8)
    if need_ctx:
        att = dense_attention(att, aq, ak, av, kvh=2, g=4, dk=HD, dv=HD, tq=L, q_off=ctx_blk, n_q=1, nk=L,
                              k_blk=ctx_blk, col_off=0, sink_col=_sink_col(sink, 2, 4, L))
        att = dense_attention(att, bq, bk, bv, kvh=8, g=1, dk=256, dv=HD, tq=L, q_off=ctx_blk, n_q=1, nk=L,
                              k_blk=ctx_blk, col_off=8)
    return att.reshape(R, D)


def mixer_cd(x2d, gain, modt, l, need_ctx, w_in, q_norm, k_norm, rpb, tabs):
    w_in = w_in.astype(BF16)
    extras = [(t, True) for t in tabs["r128"]] + [(q_norm.reshape(1, HD), False), (k_norm.reshape(1, HD), False)]
    cq, ck, cv = (t.reshape(B, T, -1) for t in project_in(
        x2d, gain, modt, l, w_in, 0, PROJ_A, 1, _cd_attn_epilogue, extras, (1024, 256, 256), "project_in_cd_attn"))
    n_d = CD_IN - PROJ_A
    q_scale = jnp.concatenate([jnp.full((1024,), QK_SCALE_LOG2, F32), jnp.ones((n_d - 1024,), F32)])
    dqkv, = project_in(x2d, gain, modt, l, w_in, 1, PROJ_A, n_d // PROJ_A, _cd_nbr_epilogue,
                       [(q_scale.reshape(n_d // PROJ_A, PROJ_A), False)], (n_d,), "project_in_cd_nbr")
    dqkv = dqkv.reshape(B, T, n_d)
    ctx_blk = S // L
    att = jnp.zeros((B, T, D), BF16)
    att = dense_attention(att, cq, ck, cv, kvh=2, g=4, dk=HD, dv=HD, tq=1024, q_off=0, n_q=S // 1024, nk=T, k_blk=0,
                          col_off=0)
    att = neighbourhood_attention(att, dqkv, dqkv, dqkv, nbr_bias_table(rpb), heads=8, col_off=8, k_col=8,
                                  v_col=16)
    if need_ctx:
        att = dense_attention(att, cq, ck, cv, kvh=2, g=4, dk=HD, dv=HD, tq=L, q_off=ctx_blk, n_q=1, nk=L,
                              k_blk=ctx_blk, col_off=0)
        att = dense_attention(att, dqkv, dqkv, dqkv, kvh=8, g=1, dk=HD, dv=HD, tq=L, q_off=ctx_blk, n_q=1, nk=L,
                              k_blk=ctx_blk, col_off=8, k_col=8, v_col=16)
    return att.reshape(R, D)


def _final_norm_body(x_ref, g_ref, o_ref):
    o_ref[0] = _rms(x_ref[0], g_ref[...])


def final_rmsnorm(x3, gain):
    return pl.pallas_call(
        _final_norm_body,
        grid=(B, S // TM),
        in_specs=[pl.BlockSpec((1, TM, D), lambda b, i: (b, i, 0)), pl.BlockSpec((1, D), lambda b, i: (0, 0))],
        out_specs=pl.BlockSpec((1, TM, D), lambda b, i: (b, i, 0)),
        out_shape=jax.ShapeDtypeStruct((B, S, D), F32),
        compiler_params=_cp("parallel", "parallel"),
        name="final_norm",
    )(x3, gain.reshape(1, D))


def kernel(x, c, ctx, c_ctx, w_ada, b_ada, norm1, norm2, ab_w_in, ab_sink, ab_q_norm, ab_w_uq, ab_kv_norm,
           ab_w_ukv, ab_w_out, cd_w_in, cd_q_norm, cd_k_norm, cd_rpb, cd_w_out, w_router, w_gate, w_up, w_down,
           final_norm):
    c8 = jnp.concatenate([c, c_ctx[None, :], jnp.zeros((8 - B - 1, D), F32)], axis=0)
    modt = ada_modulation(c8, w_ada, b_ada)
    c64, s64 = _rope_tables(64)
    zeros64 = jnp.zeros((T, 64), F32)
    tabs = {
        "r128": _rope_tables(HD),
        "r64x2": (jnp.concatenate([c64, c64], axis=1), jnp.concatenate([s64, s64], axis=1)),
        "r64pad": (jnp.concatenate([c64, zeros64], axis=1), jnp.concatenate([s64, zeros64], axis=1)),
    }
    x2d = jnp.concatenate([x, ctx], axis=1).reshape(R, D)
    for l in range(DEPTH):
        need_ctx = l < DEPTH - 1
        i = l // 2
        if l % 2 == 0:
            att = mixer_ab(x2d, norm1[l], modt, l, need_ctx, ab_w_in[i], ab_sink[i], ab_q_norm[i], ab_w_uq[i],
                           ab_kv_norm[i], ab_w_ukv[i], tabs)
            w_out = ab_w_out[i]
        else:
            att = mixer_cd(x2d, norm1[l], modt, l, need_ctx, cd_w_in[i], cd_q_norm[i], cd_k_norm[i], cd_rpb[i],
                           tabs)
            w_out = cd_w_out[i]
        x2d = matmul_gated_residual(att, w_out.astype(BF16), x2d, modt, l, 2)
        x2d = moe_block(x2d, norm2[l], modt, l, w_router[l].T, w_gate, w_up, w_down)
    return final_rmsnorm(x2d.reshape(B, T, D), final_norm)
```
